```python
import jax, jax.numpy as jnp
from jax import lax
import numpy as np

D_MODEL = 1024
BATCH = 2
SEQ = 16384
DEPTH = 2

CHUNK = 128
A_HEADS = 4
A_WIDTH = 512
A_GROUP = A_WIDTH // A_HEADS
B_WIDTH = 512
CONV_W = 3
C_HEADS = 8
C_HEAD_DIM = 64
C_WIDTH = C_HEADS * C_HEAD_DIM
Q_BLOCK = 128
N_BRANCH = 3
D_FF = 4 * D_MODEL
N_MOD = 6
IN_COLS = 2 * A_WIDTH + 3 * B_WIDTH + 3 * C_WIDTH + N_BRANCH * D_MODEL
EPS = 1e-6

kernel_name = "hybrid_gmlp_shortconv_stickbreak_block"


def rms_norm(x, g):
    xf = x.astype(jnp.float32)
    y = xf * lax.rsqrt(jnp.mean(xf * xf, axis=-1, keepdims=True) + EPS)
    return (y * g.astype(jnp.float32)).astype(x.dtype)


def layer_norm(x, g, b):
    xf = x.astype(jnp.float32)
    mu = jnp.mean(xf, axis=-1, keepdims=True)
    var = jnp.mean(jnp.square(xf - mu), axis=-1, keepdims=True)
    return ((xf - mu) * lax.rsqrt(var + EPS) * g.astype(jnp.float32) + b.astype(jnp.float32)).astype(x.dtype)


def chunked_spatial_gating(u, v, w_s, b_s, ln_g, ln_b):
    bsz, s_len, _ = v.shape
    n_chunk = s_len // CHUNK
    v = layer_norm(v, ln_g, ln_b)
    vc = v.reshape(bsz, n_chunk, CHUNK, A_HEADS, A_GROUP)
    causal = jnp.tril(jnp.ones((CHUNK, CHUNK), dtype=bool))
    w = jnp.where(causal, w_s, 0.0).astype(v.dtype)
    mixed = jnp.einsum('hts,bnshc->bnthc', w, vc) + b_s.T.astype(v.dtype)[None, None, :, :, None]
    return u * mixed.reshape(bsz, s_len, A_WIDTH)


def short_gated_conv(gate_b, gate_c, xv, conv_w):
    s_len = xv.shape[1]
    z = gate_c * xv
    zp = jnp.pad(z, ((0, 0), (CONV_W - 1, 0), (0, 0)))
    y = sum(conv_w[k].astype(z.dtype) * zp[:, k:k + s_len] for k in range(CONV_W))
    return gate_b * y


def stick_breaking_attention(q, k, v):
    bsz, s_len, n_h, d_h = q.shape
    n_blk = s_len // Q_BLOCK
    scale = d_h ** -0.5
    local = jnp.arange(Q_BLOCK)
    diag_mask = local[None, :] < local[:, None]
    outs = []
    for i in range(n_blk):
        nk = i + 1
        q_i = q[:, i * Q_BLOCK:(i + 1) * Q_BLOCK]
        k_i = k[:, :nk * Q_BLOCK].reshape(bsz, nk, Q_BLOCK, n_h, d_h)
        v_i = v[:, :nk * Q_BLOCK].reshape(bsz, nk, Q_BLOCK, n_h, d_h)
        z = jnp.einsum('bqhd,bnkhd->bhqnk', q_i, k_i).astype(jnp.float32) * scale
        mask = (jnp.arange(nk)[None, :, None] < i) | diag_mask[:, None, :]
        log_beta = jax.nn.log_sigmoid(z)
        log_one_minus = jnp.where(mask, log_beta - z, 0.0)
        within = lax.cumsum(log_one_minus, axis=4, reverse=True) - log_one_minus
        blk_tot = jnp.sum(log_one_minus, axis=4)
        across = lax.cumsum(blk_tot, axis=3, reverse=True) - blk_tot
        a = jnp.where(mask, jnp.exp(log_beta + within + across[..., None]), 0.0)
        outs.append(jnp.einsum('bhqnk,bnkhd->bqhd', a.astype(v.dtype), v_i))
    return jnp.concatenate(outs, axis=1).reshape(bsz, s_len, n_h * d_h)


def mixer_sublayer(h, w_in, a_ln_g, a_ln_b, a_ws, a_bs, b_conv_w, w_a_out, w_b_out, w_c_out, w_o):
    bsz, s_len, _ = h.shape
    proj = h @ w_in
    sizes = [A_WIDTH, A_WIDTH, B_WIDTH, B_WIDTH, B_WIDTH, C_WIDTH, C_WIDTH, C_WIDTH]
    idx = np.cumsum(sizes).tolist()
    a_u, a_v, b_b, b_c, b_x, c_q, c_k, c_v, gates = jnp.split(proj, idx, axis=-1)
    y_a = chunked_spatial_gating(jax.nn.gelu(a_u), jax.nn.gelu(a_v), a_ws, a_bs, a_ln_g, a_ln_b) @ w_a_out
    y_b = short_gated_conv(b_b, b_c, b_x, b_conv_w) @ w_b_out
    q = c_q.reshape(bsz, s_len, C_HEADS, C_HEAD_DIM)
    k = c_k.reshape(bsz, s_len, C_HEADS, C_HEAD_DIM)
    v = c_v.reshape(bsz, s_len, C_HEADS, C_HEAD_DIM)
    y_c = stick_breaking_attention(q, k, v) @ w_c_out
    g = jax.nn.sigmoid(gates.reshape(bsz, s_len, N_BRANCH, D_MODEL).astype(jnp.float32)).astype(h.dtype)
    merged = g[:, :, 0] * y_a + g[:, :, 1] * y_b + g[:, :, 2] * y_c
    return merged @ w_o


def setup_inputs(seed: int = 0) -> dict:
    key = jax.random.key(seed)
    ks = jax.random.split(key, 20)
    f32 = jnp.float32
    nrm = lambda k, shape, s: jax.random.normal(k, shape, f32) * s
    return {
        "x": nrm(ks[0], (BATCH, SEQ, D_MODEL), 1.0),
        "c": nrm(ks[1], (BATCH, D_MODEL), 1.0),
        "ada_w": nrm(ks[2], (DEPTH, D_MODEL, N_MOD * D_MODEL), 0.3 * D_MODEL ** -0.5),
        "ada_b": nrm(ks[3], (DEPTH, N_MOD * D_MODEL), 0.02),
        "norm1_g": 1.0 + nrm(ks[4], (DEPTH, D_MODEL), 0.02),
        "w_in": nrm(ks[5], (DEPTH, D_MODEL, IN_COLS), D_MODEL ** -0.5),
        "a_ln_g": 1.0 + nrm(ks[6], (DEPTH, A_WIDTH), 0.02),
        "a_ln_b": nrm(ks[7], (DEPTH, A_WIDTH), 0.02),
        "a_ws": nrm(ks[8], (DEPTH, A_HEADS, CHUNK, CHUNK), CHUNK ** -0.5),
        "a_bs": 1.0 + nrm(ks[9], (DEPTH, A_HEADS, CHUNK), 0.1),
        "b_conv_w": nrm(ks[10], (DEPTH, CONV_W, B_WIDTH), CONV_W ** -0.5),
        "w_a_out": nrm(ks[11], (DEPTH, A_WIDTH, D_MODEL), A_WIDTH ** -0.5),
        "w_b_out": nrm(ks[12], (DEPTH, B_WIDTH, D_MODEL), B_WIDTH ** -0.5),
        "w_c_out": nrm(ks[13], (DEPTH, C_WIDTH, D_MODEL), C_WIDTH ** -0.5),
        "w_o": nrm(ks[14], (DEPTH, D_MODEL, D_MODEL), D_MODEL ** -0.5),
        "norm2_g": 1.0 + nrm(ks[15], (DEPTH, D_MODEL), 0.02),
        "w_ff1": nrm(ks[16], (DEPTH, D_MODEL, D_FF), D_MODEL ** -0.5),
        "w_ff2": nrm(ks[17], (DEPTH, D_FF, D_MODEL), D_FF ** -0.5),
        "final_g": 1.0 + nrm(ks[18], (D_MODEL,), 0.02),
    }


def reference(x, c, ada_w, ada_b, norm1_g, w_in, a_ln_g, a_ln_b, a_ws, a_bs, b_conv_w,
              w_a_out, w_b_out, w_c_out, w_o, norm2_g, w_ff1, w_ff2, final_g):
    c_act = jax.nn.silu(c)
    for l in range(DEPTH):
        mod = c_act @ ada_w[l] + ada_b[l]
        shift1, scale1, gate1, shift2, scale2, gate2 = jnp.split(mod[:, None, :], N_MOD, axis=-1)
        h = rms_norm(x, norm1_g[l]) * (1.0 + scale1) + shift1
        y = mixer_sublayer(h, w_in[l], a_ln_g[l], a_ln_b[l], a_ws[l], a_bs[l], b_conv_w[l],
                           w_a_out[l], w_b_out[l], w_c_out[l], w_o[l])
        x = x + (1.0 + gate1) * y
        h = rms_norm(x, norm2_g[l]) * (1.0 + scale2) + shift2
        y = jnp.square(jax.nn.relu(h @ w_ff1[l])) @ w_ff2[l]
        x = x + (1.0 + gate2) * y
    return rms_norm(x, final_g)
```

```python
import functools

import jax
import jax.numpy as jnp
from jax import lax
from jax.experimental import pallas as pl
from jax.experimental.pallas import tpu as pltpu

D_MODEL = 1024
N_MOD = 6
A_WIDTH = 512
A_HEADS = 4
CHUNK = 128
B_WIDTH = 512
CONV_W = 3
C_WIDTH = 512
C_HEAD_DIM = 64
Q_BLOCK = 128
D_FF = 4 * D_MODEL
EPS = 1e-6

LANES = 128
SUBLANES = 8
HEAD_PAIRS = C_WIDTH // LANES
VMEM_LIMIT_BYTES = 56 * 1024 * 1024
F32_EXP_UNDERFLOW = -104.0

_OFF_AU = 0
_OFF_AV = _OFF_AU + A_WIDTH
_OFF_BB = _OFF_AV + A_WIDTH
_OFF_BC = _OFF_BB + B_WIDTH
_OFF_BX = _OFF_BC + B_WIDTH
_OFF_Q = _OFF_BX + B_WIDTH
_OFF_K = _OFF_Q + C_WIDTH
_OFF_V = _OFF_K + C_WIDTH
_OFF_G = _OFF_V + C_WIDTH

F32 = jnp.float32
BF16 = jnp.bfloat16


def _dot(a, b):
    return jnp.dot(a, b, preferred_element_type=F32)


def _rms_norm(x, g):
    return x * lax.rsqrt(jnp.mean(x * x, axis=-1, keepdims=True) + EPS) * g


def _resident(block_shape, index_map):
    return pl.BlockSpec(block_shape, index_map, pipeline_mode=pl.Buffered(1))


def _ada_kernel(c_ref, w_ref, b_ref, o_ref):
    c = c_ref[...]
    c_act = c * jax.nn.sigmoid(c)
    o_ref[0] = _dot(c_act.astype(BF16), w_ref[0].astype(BF16)) + b_ref[0]


def _ada_call(c_pad, ada_w, ada_b):
    depth = ada_w.shape[0]
    rows = c_pad.shape[0]
    return pl.pallas_call(
        _ada_kernel,
        grid=(depth, N_MOD),
        in_specs=[
            pl.BlockSpec((rows, D_MODEL), lambda l, j: (0, 0)),
            pl.BlockSpec((1, D_MODEL, D_MODEL), lambda l, j: (l, 0, j)),
            pl.BlockSpec((1, 1, D_MODEL), lambda l, j: (l, 0, j)),
        ],
        out_specs=pl.BlockSpec((1, rows, D_MODEL), lambda l, j: (l, 0, j)),
        out_shape=jax.ShapeDtypeStruct((depth, rows, N_MOD * D_MODEL), F32),
        compiler_params=pltpu.CompilerParams(
            dimension_semantics=("arbitrary", "arbitrary"),
            vmem_limit_bytes=VMEM_LIMIT_BYTES),
        name="adaln",
    )(c_pad, ada_w, ada_b.reshape(depth, 1, N_MOD * D_MODEL))


def _mixer_in_kernel(x_ref, mod_ref, n1g_ref, w_in_ref, lng_ref, lnb_ref, ws_ref, bsb_ref,
                     conv_ref, wa_ref, wb_ref,
                     q_ref, k_ref, v_ref, part_ref, g2_ref,
                     ya_ref, halo_ref, *, tm):
    s_blk = pl.program_id(1)
    x = x_ref[0]
    mod = mod_ref[0]
    shift1 = mod[:, 0:D_MODEL]
    scale1 = mod[:, D_MODEL:2 * D_MODEL]
    h = _rms_norm(x, n1g_ref[...]) * (1.0 + scale1) + shift1
    hb = h.astype(BF16)

    def proj(off, width):
        return _dot(hb, w_in_ref[:, off:off + width])

    u = jax.nn.gelu(proj(_OFF_AU, A_WIDTH))
    v = jax.nn.gelu(proj(_OFF_AV, A_WIDTH))
    mu = jnp.mean(v, axis=-1, keepdims=True)
    var = jnp.mean(jnp.square(v - mu), axis=-1, keepdims=True)
    v = ((v - mu) * lax.rsqrt(var + EPS) * lng_ref[...] + lnb_ref[...]).astype(BF16)
    t_idx = lax.broadcasted_iota(jnp.int32, (CHUNK, CHUNK), 0)
    s_idx = lax.broadcasted_iota(jnp.int32, (CHUNK, CHUNK), 1)
    causal = s_idx <= t_idx
    group = A_WIDTH // A_HEADS
    for hd in range(A_HEADS):
        w_h = jnp.where(causal, ws_ref[hd], 0.0).astype(BF16)
        cols = slice(hd * group, (hd + 1) * group)
        for ck in range(tm // CHUNK):
            rows = slice(ck * CHUNK, (ck + 1) * CHUNK)
            mixed = _dot(w_h, v[rows, cols]) + bsb_ref[hd]
            ya_ref[rows, cols] = (u[rows, cols] * mixed).astype(BF16)
    y_a = _dot(ya_ref[...], wa_ref[...])

    b_b = proj(_OFF_BB, B_WIDTH)
    z = proj(_OFF_BC, B_WIDTH) * proj(_OFF_BX, B_WIDTH)

    @pl.when(s_blk == 0)
    def _():
        halo_ref[...] = jnp.zeros_like(halo_ref)

    prev1 = halo_ref[SUBLANES - 1:SUBLANES, :]
    prev2 = halo_ref[SUBLANES - 2:SUBLANES - 1, :]
    row = lax.broadcasted_iota(jnp.int32, (tm, B_WIDTH), 0)
    z1 = jnp.where(row == 0, prev1, pltpu.roll(z, 1, 0))
    z2 = jnp.where(row == 0, prev2, jnp.where(row == 1, prev1, pltpu.roll(z, 2, 0)))
    halo_ref[...] = z[tm - SUBLANES:, :]
    conv = conv_ref[0:1, :] * z2 + conv_ref[1:2, :] * z1 + conv_ref[2:3, :] * z
    y_b = _dot((b_b * conv).astype(BF16), wb_ref[...])

    q = proj(_OFF_Q, C_WIDTH) * (C_HEAD_DIM ** -0.5)
    k = proj(_OFF_K, C_WIDTH)
    vv = proj(_OFF_V, C_WIDTH)
    for p in range(HEAD_PAIRS):
        cols = slice(p * LANES, (p + 1) * LANES)
        q_ref[0, p] = q[:, cols].astype(BF16)
        k_ref[0, p] = k[:, cols].astype(BF16)
        v_ref[0, p] = vv[:, cols].astype(BF16)

    g0 = jax.nn.sigmoid(proj(_OFF_G, D_MODEL))
    g1 = jax.nn.sigmoid(proj(_OFF_G + D_MODEL, D_MODEL))
    part_ref[0] = g0 * y_a + g1 * y_b
    g2_ref[0] = jax.nn.sigmoid(proj(_OFF_G + 2 * D_MODEL, D_MODEL))


def _mixer_in_call(x, mod, n1g, w_in, lng, lnb, ws, bsb, conv_w, wa, wb, *, tm):
    bsz, seq, _ = x.shape
    in_cols = w_in.shape[1]
    const2 = lambda b, s: (0, 0)
    const3 = lambda b, s: (0, 0, 0)
    tok = lambda b, s: (b, s, 0)
    hp = lambda b, s: (b, 0, s, 0)
    qkv_shape = jax.ShapeDtypeStruct((bsz, HEAD_PAIRS, seq, LANES), BF16)
    return pl.pallas_call(
        functools.partial(_mixer_in_kernel, tm=tm),
        grid=(bsz, seq // tm),
        in_specs=[
            pl.BlockSpec((1, tm, D_MODEL), tok),
            pl.BlockSpec((1, 1, N_MOD * D_MODEL), lambda b, s: (b, 0, 0)),
            _resident((1, D_MODEL), const2),
            _resident((D_MODEL, in_cols), const2),
            _resident((1, A_WIDTH), const2),
            _resident((1, A_WIDTH), const2),
            _resident((A_HEADS, CHUNK, CHUNK), const3),
            _resident((A_HEADS, CHUNK, A_WIDTH // A_HEADS), const3),
            _resident((CONV_W, B_WIDTH), const2),
            _resident((A_WIDTH, D_MODEL), const2),
            _resident((B_WIDTH, D_MODEL), const2),
        ],
        out_specs=[
            pl.BlockSpec((1, HEAD_PAIRS, tm, LANES), hp),
            pl.BlockSpec((1, HEAD_PAIRS, tm, LANES), hp),
            pl.BlockSpec((1, HEAD_PAIRS, tm, LANES), hp),
            pl.BlockSpec((1, tm, D_MODEL), tok),
            pl.BlockSpec((1, tm, D_MODEL), tok),
        ],
        out_shape=[qkv_shape, qkv_shape, qkv_shape,
                   jax.ShapeDtypeStruct((bsz, seq, D_MODEL), F32),
                   jax.ShapeDtypeStruct((bsz, seq, D_MODEL), F32)],
        scratch_shapes=[pltpu.VMEM((tm, A_WIDTH), BF16),
                        pltpu.VMEM((SUBLANES, B_WIDTH), F32)],
        compiler_params=pltpu.CompilerParams(
            dimension_semantics=("arbitrary", "arbitrary"),
            vmem_limit_bytes=VMEM_LIMIT_BYTES),
        name="mixer_in",
    )(x, mod, n1g, w_in, lng, lnb, ws, bsb, conv_w, wa, wb)


def _split_dot(x, w):
    hi = x.astype(BF16)
    lo = (x - hi.astype(F32)).astype(BF16)
    return _dot(hi, w) + _dot(lo, w)


def _attn_kernel(q_ref, k_ref, v_ref, scan_ref, o_ref, acc_ref, across_ref):
    i = pl.program_id(2)
    q = q_ref[0, 0]
    first_head = lax.broadcasted_iota(jnp.int32, (Q_BLOCK, LANES), 1) < C_HEAD_DIM
    zero = jnp.zeros((Q_BLOCK, LANES), BF16)

    def head_stack(ref, j):
        blk = ref[0, 0, pl.ds(pl.multiple_of(j * Q_BLOCK, Q_BLOCK), Q_BLOCK), :]
        return jnp.concatenate([jnp.where(first_head, blk, zero),
                                jnp.where(first_head, zero, blk)], axis=0)

    def log_terms(j):
        z = lax.dot_general(q, head_stack(k_ref, j), (((1,), (1,)), ((), ())),
                            preferred_element_type=F32)
        log_beta = jnp.minimum(z, 0.0) - jnp.log1p(jnp.exp(-jnp.abs(z)))
        return log_beta, log_beta - z

    row = lax.broadcasted_iota(jnp.int32, (Q_BLOCK, 2 * Q_BLOCK), 0)
    col = lax.broadcasted_iota(jnp.int32, (Q_BLOCK, 2 * Q_BLOCK), 1) & (Q_BLOCK - 1)
    mask = col < row
    log_beta, log_om = log_terms(i)
    log_om = jnp.where(mask, log_om, 0.0)
    scan = _split_dot(log_om, scan_ref[...])
    a = jnp.where(mask, jnp.exp(log_beta + scan[:, :2 * Q_BLOCK]), 0.0)
    acc_ref[...] = _dot(a.astype(BF16), head_stack(v_ref, i))
    across_ref[...] = scan[:, 2 * Q_BLOCK:]

    def body(carry):
        j, _ = carry
        log_beta, log_om = log_terms(j)
        scan = _split_dot(log_om, scan_ref[...])
        across = across_ref[...]
        a = jnp.exp(log_beta + scan[:, :2 * Q_BLOCK] + across)
        acc_ref[...] += _dot(a.astype(BF16), head_stack(v_ref, j))
        across = across + scan[:, 2 * Q_BLOCK:]
        across_ref[...] = across
        return j - 1, (jnp.max(across) < F32_EXP_UNDERFLOW).astype(jnp.int32)

    lax.while_loop(lambda c: (c[0] >= 0) & (c[1] == 0), body, (i - 1, jnp.int32(0)))
    o_ref[0, 0] = acc_ref[...].astype(BF16)


def _attn_call(q, k, v, scan_mat):
    bsz, _, seq, _ = q.shape
    return pl.pallas_call(
        _attn_kernel,
        grid=(bsz, HEAD_PAIRS, seq // Q_BLOCK),
        in_specs=[
            pl.BlockSpec((1, 1, Q_BLOCK, LANES), lambda b, p, i: (b, p, i, 0)),
            pl.BlockSpec((1, 1, seq, LANES), lambda b, p, i: (b, p, 0, 0)),
            pl.BlockSpec((1, 1, seq, LANES), lambda b, p, i: (b, p, 0, 0)),
            _resident(scan_mat.shape, lambda b, p, i: (0, 0)),
        ],
        out_specs=pl.BlockSpec((1, 1, Q_BLOCK, LANES), lambda b, p, i: (b, p, i, 0)),
        out_shape=jax.ShapeDtypeStruct(q.shape, BF16),
        scratch_shapes=[pltpu.VMEM((Q_BLOCK, LANES), F32),
                        pltpu.VMEM((Q_BLOCK, 2 * Q_BLOCK), F32)],
        compiler_params=pltpu.CompilerParams(
            dimension_semantics=("arbitrary", "arbitrary", "arbitrary"),
            vmem_limit_bytes=VMEM_LIMIT_BYTES),
        name="attention",
    )(q, k, v, scan_mat)


def _scan_matrix():
    r = jnp.arange(2 * Q_BLOCK)
    same_head = (r[:, None] // Q_BLOCK) == (r[None, :] // Q_BLOCK)
    suffix = same_head & (r[:, None] > r[None, :])
    return jnp.concatenate([suffix, same_head], axis=1).astype(BF16)


def _mixer_out_kernel(x_ref, att_ref, part_ref, g2_ref, mod_ref, wc_ref, wo_ref, n2g_ref,
                      w1_ref, w2_ref, fg_ref, o_ref, *, ff_chunk, final):
    mod = mod_ref[0]
    gate1 = mod[:, 2 * D_MODEL:3 * D_MODEL]
    shift2 = mod[:, 3 * D_MODEL:4 * D_MODEL]
    scale2 = mod[:, 4 * D_MODEL:5 * D_MODEL]
    gate2 = mod[:, 5 * D_MODEL:6 * D_MODEL]
    att = jnp.concatenate([att_ref[0, p] for p in range(HEAD_PAIRS)], axis=1)
    merged = part_ref[0] + g2_ref[0] * _dot(att, wc_ref[...])
    x1 = x_ref[0] + (1.0 + gate1) * _dot(merged.astype(BF16), wo_ref[...])
    hb = (_rms_norm(x1, n2g_ref[...]) * (1.0 + scale2) + shift2).astype(BF16)
    y = None
    for c in range(D_FF // ff_chunk):
        cols = slice(c * ff_chunk, (c + 1) * ff_chunk)
        hid = jnp.square(jnp.maximum(_dot(hb, w1_ref[:, cols]), 0.0)).astype(BF16)
        part = _dot(hid, w2_ref[cols, :])
        y = part if y is None else y + part
    x2 = x1 + (1.0 + gate2) * y
    if final:
        x2 = _rms_norm(x2, fg_ref[...])
    o_ref[0] = x2


def _mixer_out_call(x, att, part, g2, mod, wc, wo, n2g, w1, w2, fg, *, tm, final):
    bsz, seq, _ = x.shape
    const2 = lambda b, s: (0, 0)
    tok = lambda b, s: (b, s, 0)
    return pl.pallas_call(
        functools.partial(_mixer_out_kernel, ff_chunk=D_MODEL, final=final),
        grid=(bsz, seq // tm),
        in_specs=[
            pl.BlockSpec((1, tm, D_MODEL), tok),
            pl.BlockSpec((1, HEAD_PAIRS, tm, LANES), lambda b, s: (b, 0, s, 0)),
            pl.BlockSpec((1, tm, D_MODEL), tok),
            pl.BlockSpec((1, tm, D_MODEL), tok),
            pl.BlockSpec((1, 1, N_MOD * D_MODEL), lambda b, s: (b, 0, 0)),
            _resident((C_WIDTH, D_MODEL), const2),
            _resident((D_MODEL, D_MODEL), const2),
            _resident((1, D_MODEL), const2),
            _resident((D_MODEL, D_FF), const2),
            _resident((D_FF, D_MODEL), const2),
            _resident((1, D_MODEL), const2),
        ],
        out_specs=pl.BlockSpec((1, tm, D_MODEL), tok),
        out_shape=jax.ShapeDtypeStruct((bsz, seq, D_MODEL), F32),
        compiler_params=pltpu.CompilerParams(
            dimension_semantics=("arbitrary", "arbitrary"),
            vmem_limit_bytes=VMEM_LIMIT_BYTES),
        name="mixer_out",
    )(x, att, part, g2, mod, wc, wo, n2g, w1, w2, fg)


def kernel(x, c, ada_w, ada_b, norm1_g, w_in, a_ln_g, a_ln_b, a_ws, a_bs, b_conv_w,
           w_a_out, w_b_out, w_c_out, w_o, norm2_g, w_ff1, w_ff2, final_g):
    depth = ada_w.shape[0]
    bsz = x.shape[0]
    tm = 256
    c_pad = jnp.zeros((SUBLANES, D_MODEL), F32).at[:bsz].set(c)
    mod_all = _ada_call(c_pad, ada_w, ada_b)
    scan_mat = _scan_matrix()
    fg = final_g.reshape(1, D_MODEL)
    for l in range(depth):
        mod = mod_all[l, :bsz].reshape(bsz, 1, N_MOD * D_MODEL)
        bsb = jnp.broadcast_to(a_bs[l][:, :, None], (A_HEADS, CHUNK, A_WIDTH // A_HEADS))
        q, k, v, part, g2 = _mixer_in_call(
            x, mod, norm1_g[l].reshape(1, D_MODEL), w_in[l].astype(BF16),
            a_ln_g[l].reshape(1, A_WIDTH), a_ln_b[l].reshape(1, A_WIDTH), a_ws[l], bsb,
            b_conv_w[l], w_a_out[l].astype(BF16), w_b_out[l].astype(BF16), tm=tm)
        att = _attn_call(q, k, v, scan_mat)
        x = _mixer_out_call(
            x, att, part, g2, mod, w_c_out[l].astype(BF16), w_o[l].astype(BF16),
            norm2_g[l].reshape(1, D_MODEL), w_ff1[l].astype(BF16), w_ff2[l].astype(BF16),
            fg, tm=tm, final=(l == depth - 1))
    return x
```

```python
import functools

import jax
import jax.numpy as jnp
from jax import lax
from jax.experimental import pallas as pl
from jax.experimental.pallas import tpu as pltpu

D_MODEL = 1024
N_MOD = 6
A_WIDTH = 512
A_HEADS = 4
CHUNK = 128
B_WIDTH = 512
CONV_W = 3
C_WIDTH = 512
C_HEAD_DIM = 64
Q_BLOCK = 128
D_FF = 4 * D_MODEL
EPS = 1e-6

LANES = 128
SUBLANES = 8
HEAD_PAIRS = C_WIDTH // LANES
VMEM_LIMIT_BYTES = 56 * 1024 * 1024
F32_EXP_UNDERFLOW = 104.0

_OFF_AU = 0
_OFF_AV = _OFF_AU + A_WIDTH
_OFF_BB = _OFF_AV + A_WIDTH
_OFF_BC = _OFF_BB + B_WIDTH
_OFF_BX = _OFF_BC + B_WIDTH
_OFF_Q = _OFF_BX + B_WIDTH
_OFF_K = _OFF_Q + C_WIDTH
_OFF_V = _OFF_K + C_WIDTH
_OFF_G = _OFF_V + C_WIDTH

F32 = jnp.float32
BF16 = jnp.bfloat16


def _dot(a, b):
    return jnp.dot(a, b, preferred_element_type=F32)


def _rms_norm(x, g):
    return x * lax.rsqrt(jnp.mean(x * x, axis=-1, keepdims=True) + EPS) * g


def _resident(block_shape, index_map):
    return pl.BlockSpec(block_shape, index_map, pipeline_mode=pl.Buffered(1))


def _ada_kernel(c_ref, w_ref, b_ref, o_ref):
    c = c_ref[...]
    c_act = c * jax.nn.sigmoid(c)
    o_ref[0] = _dot(c_act.astype(BF16), w_ref[0].astype(BF16)) + b_ref[0]


def _ada_call(c_pad, ada_w, ada_b):
    depth = ada_w.shape[0]
    rows = c_pad.shape[0]
    return pl.pallas_call(
        _ada_kernel,
        grid=(depth, N_MOD),
        in_specs=[
            pl.BlockSpec((rows, D_MODEL), lambda l, j: (0, 0)),
            pl.BlockSpec((1, D_MODEL, D_MODEL), lambda l, j: (l, 0, j)),
            pl.BlockSpec((1, 1, D_MODEL), lambda l, j: (l, 0, j)),
        ],
        out_specs=pl.BlockSpec((1, rows, D_MODEL), lambda l, j: (l, 0, j)),
        out_shape=jax.ShapeDtypeStruct((depth, rows, N_MOD * D_MODEL), F32),
        compiler_params=pltpu.CompilerParams(
            dimension_semantics=("arbitrary", "arbitrary"),
            vmem_limit_bytes=VMEM_LIMIT_BYTES),
        name="adaln",
    )(c_pad, ada_w, ada_b.reshape(depth, 1, N_MOD * D_MODEL))


def _mixer_in_kernel(x_ref, mod_ref, n1g_ref, w_in_ref, lng_ref, lnb_ref, ws_ref, bsb_ref,
                     conv_ref, wa_ref, wb_ref,
                     q_ref, k_ref, v_ref, part_ref, g2_ref,
                     ya_ref, halo_ref, *, tm):
    s_blk = pl.program_id(1)
    x = x_ref[0]
    mod = mod_ref[0]
    shift1 = mod[:, 0:D_MODEL]
    scale1 = mod[:, D_MODEL:2 * D_MODEL]
    h = _rms_norm(x, n1g_ref[...]) * (1.0 + scale1) + shift1
    hb = h.astype(BF16)

    def proj(off, width):
        return _dot(hb, w_in_ref[:, off:off + width])

    u = jax.nn.gelu(proj(_OFF_AU, A_WIDTH))
    v = jax.nn.gelu(proj(_OFF_AV, A_WIDTH))
    mu = jnp.mean(v, axis=-1, keepdims=True)
    var = jnp.mean(jnp.square(v - mu), axis=-1, keepdims=True)
    v = ((v - mu) * lax.rsqrt(var + EPS) * lng_ref[...] + lnb_ref[...]).astype(BF16)
    t_idx = lax.broadcasted_iota(jnp.int32, (CHUNK, CHUNK), 0)
    s_idx = lax.broadcasted_iota(jnp.int32, (CHUNK, CHUNK), 1)
    causal = s_idx <= t_idx
    group = A_WIDTH // A_HEADS
    for hd in range(A_HEADS):
        w_h = jnp.where(causal, ws_ref[hd], 0.0).astype(BF16)
        cols = slice(hd * group, (hd + 1) * group)
        for ck in range(tm // CHUNK):
            rows = slice(ck * CHUNK, (ck + 1) * CHUNK)
            mixed = _dot(w_h, v[rows, cols]) + bsb_ref[hd]
            ya_ref[rows, cols] = (u[rows, cols] * mixed).astype(BF16)
    y_a = _dot(ya_ref[...], wa_ref[...])

    b_b = proj(_OFF_BB, B_WIDTH)
    z = proj(_OFF_BC, B_WIDTH) * proj(_OFF_BX, B_WIDTH)

    @pl.when(s_blk == 0)
    def _():
        halo_ref[...] = jnp.zeros_like(halo_ref)

    prev1 = halo_ref[SUBLANES - 1:SUBLANES, :]
    prev2 = halo_ref[SUBLANES - 2:SUBLANES - 1, :]
    row = lax.broadcasted_iota(jnp.int32, (tm, B_WIDTH), 0)
    z1 = jnp.where(row == 0, prev1, pltpu.roll(z, 1, 0))
    z2 = jnp.where(row == 0, prev2, jnp.where(row == 1, prev1, pltpu.roll(z, 2, 0)))
    halo_ref[...] = z[tm - SUBLANES:, :]
    conv = conv_ref[0:1, :] * z2 + conv_ref[1:2, :] * z1 + conv_ref[2:3, :] * z
    y_b = _dot((b_b * conv).astype(BF16), wb_ref[...])

    q = proj(_OFF_Q, C_WIDTH) * (C_HEAD_DIM ** -0.5)
    k = proj(_OFF_K, C_WIDTH)
    vv = proj(_OFF_V, C_WIDTH)
    for p in range(HEAD_PAIRS):
        cols = slice(p * LANES, (p + 1) * LANES)
        q_ref[0, p] = q[:, cols].astype(BF16)
        k_ref[0, p] = k[:, cols].astype(BF16)
        v_ref[0, p] = vv[:, cols].astype(BF16)

    g0 = jax.nn.sigmoid(proj(_OFF_G, D_MODEL))
    g1 = jax.nn.sigmoid(proj(_OFF_G + D_MODEL, D_MODEL))
    part_ref[0] = g0 * y_a + g1 * y_b
    g2_ref[0] = jax.nn.sigmoid(proj(_OFF_G + 2 * D_MODEL, D_MODEL))


def _mixer_in_call(x, mod, n1g, w_in, lng, lnb, ws, bsb, conv_w, wa, wb, *, tm):
    bsz, seq, _ = x.shape
    in_cols = w_in.shape[1]
    const2 = lambda b, s: (0, 0)
    const3 = lambda b, s: (0, 0, 0)
    tok = lambda b, s: (b, s, 0)
    hp = lambda b, s: (b, 0, s, 0)
    qkv_shape = jax.ShapeDtypeStruct((bsz, HEAD_PAIRS, seq, LANES), BF16)
    return pl.pallas_call(
        functools.partial(_mixer_in_kernel, tm=tm),
        grid=(bsz, seq // tm),
        in_specs=[
            pl.BlockSpec((1, tm, D_MODEL), tok),
            pl.BlockSpec((1, 1, N_MOD * D_MODEL), lambda b, s: (b, 0, 0)),
            _resident((1, D_MODEL), const2),
            _resident((D_MODEL, in_cols), const2),
            _resident((1, A_WIDTH), const2),
            _resident((1, A_WIDTH), const2),
            _resident((A_HEADS, CHUNK, CHUNK), const3),
            _resident((A_HEADS, CHUNK, A_WIDTH // A_HEADS), const3),
            _resident((CONV_W, B_WIDTH), const2),
            _resident((A_WIDTH, D_MODEL), const2),
            _resident((B_WIDTH, D_MODEL), const2),
        ],
        out_specs=[
            pl.BlockSpec((1, HEAD_PAIRS, tm, LANES), hp),
            pl.BlockSpec((1, HEAD_PAIRS, tm, LANES), hp),
            pl.BlockSpec((1, HEAD_PAIRS, tm, LANES), hp),
            pl.BlockSpec((1, tm, D_MODEL), tok),
            pl.BlockSpec((1, tm, D_MODEL), tok),
        ],
        out_shape=[qkv_shape, qkv_shape, qkv_shape,
                   jax.ShapeDtypeStruct((bsz, seq, D_MODEL), F32),
                   jax.ShapeDtypeStruct((bsz, seq, D_MODEL), F32)],
        scratch_shapes=[pltpu.VMEM((tm, A_WIDTH), BF16),
                        pltpu.VMEM((SUBLANES, B_WIDTH), F32)],
        compiler_params=pltpu.CompilerParams(
            dimension_semantics=("arbitrary", "arbitrary"),
            vmem_limit_bytes=VMEM_LIMIT_BYTES),
        name="mixer_in",
    )(x, mod, n1g, w_in, lng, lnb, ws, bsb, conv_w, wa, wb)


PROLOGUE_BLOCKS = 3


def _attn_kernel(q_ref, k_ref, v_ref, scan_ref, o_ref, acc_ref, across_ref):
    i = pl.program_id(1)
    first_head = lax.broadcasted_iota(jnp.int32, (Q_BLOCK, LANES), 1) < C_HEAD_DIM
    zero = jnp.zeros((Q_BLOCK, LANES), BF16)
    row = lax.broadcasted_iota(jnp.int32, (Q_BLOCK, 2 * Q_BLOCK), 0)
    col = lax.broadcasted_iota(jnp.int32, (Q_BLOCK, 2 * Q_BLOCK), 1) & (Q_BLOCK - 1)
    diag_mask = col < row
    n_rows = HEAD_PAIRS * Q_BLOCK

    def head_stack(ref, p, j):
        blk = ref[0, p, pl.ds(pl.multiple_of(j * Q_BLOCK, Q_BLOCK), Q_BLOCK), :]
        return jnp.concatenate([jnp.where(first_head, blk, zero),
                                jnp.where(first_head, zero, blk)], axis=0)

    def visit(j, diag):
        log_betas, sps = [], []
        for p in range(HEAD_PAIRS):
            z = lax.dot_general(q_ref[0, p], head_stack(k_ref, p, j),
                                (((1,), (1,)), ((), ())), preferred_element_type=F32)
            sp = jnp.maximum(z, 0.0) + jnp.log(1.0 + jnp.exp(-jnp.abs(z)))
            log_betas.append(z - sp)
            sps.append(jnp.where(diag_mask, sp, 0.0) if diag else sp)
        sp_all = jnp.concatenate(sps, axis=0)
        hi = sp_all.astype(BF16)
        lo = (sp_all - hi.astype(F32)).astype(BF16)
        scan = _dot(jnp.concatenate([hi, lo], axis=0), scan_ref[...])
        scan = scan[:n_rows] + scan[n_rows:]
        outs = []
        for p in range(HEAD_PAIRS):
            rows = slice(p * Q_BLOCK, (p + 1) * Q_BLOCK)
            a = jnp.exp(log_betas[p] - scan[rows, :2 * Q_BLOCK])
            if diag:
                a = jnp.where(diag_mask, a, 0.0)
            outs.append((_dot(a.astype(BF16), head_stack(v_ref, p, j)),
                         scan[rows, 2 * Q_BLOCK:]))
        return outs

    def all_underflowed(across):
        lowest = functools.reduce(jnp.minimum, across)
        return (jnp.min(lowest) > F32_EXP_UNDERFLOW).astype(jnp.int32)

    visits = [visit(i, True)] + [visit(jnp.maximum(i - d, 0), False)
                                 for d in range(1, PROLOGUE_BLOCKS)]
    across_all = []
    for p in range(HEAD_PAIRS):
        acc, across = visits[0][p]
        for d in range(1, PROLOGUE_BLOCKS):
            av, tot = visits[d][p]
            valid = jnp.where(i >= d, 1.0, 0.0)
            acc = acc + (jnp.exp(-across) * valid) * av
            across = across + valid * tot
        acc_ref[p] = acc
        across_ref[p] = across
        across_all.append(across)

    def body(carry):
        j, _ = carry
        across_all = []
        for p, (av, tot) in enumerate(visit(j, False)):
            across = across_ref[p]
            acc_ref[p] += jnp.exp(-across) * av
            across_ref[p] = across + tot
            across_all.append(across + tot)
        return j - 1, all_underflowed(across_all)

    lax.while_loop(lambda c: (c[0] >= 0) & (c[1] == 0), body,
                   (i - PROLOGUE_BLOCKS, all_underflowed(across_all)))
    for p in range(HEAD_PAIRS):
        o_ref[0, p] = acc_ref[p].astype(BF16)


def _attn_call(q, k, v, scan_mat):
    bsz, _, seq, _ = q.shape
    blk = lambda b, i: (b, 0, i, 0)
    whole = lambda b, i: (b, 0, 0, 0)
    return pl.pallas_call(
        _attn_kernel,
        grid=(bsz, seq // Q_BLOCK),
        in_specs=[
            pl.BlockSpec((1, HEAD_PAIRS, Q_BLOCK, LANES), blk),
            _resident((1, HEAD_PAIRS, seq, LANES), whole),
            _resident((1, HEAD_PAIRS, seq, LANES), whole),
            _resident(scan_mat.shape, lambda b, i: (0, 0)),
        ],
        out_specs=pl.BlockSpec((1, HEAD_PAIRS, Q_BLOCK, LANES), blk),
        out_shape=jax.ShapeDtypeStruct(q.shape, BF16),
        scratch_shapes=[pltpu.VMEM((HEAD_PAIRS, Q_BLOCK, LANES), F32),
                        pltpu.VMEM((HEAD_PAIRS, Q_BLOCK, LANES), F32)],
        compiler_params=pltpu.CompilerParams(
            dimension_semantics=("arbitrary", "arbitrary"),
            vmem_limit_bytes=VMEM_LIMIT_BYTES),
        name="attention",
    )(q, k, v, scan_mat)


def _scan_matrix():
    r = jnp.arange(2 * Q_BLOCK)
    same_head = (r[:, None] // Q_BLOCK) == (r[None, :] // Q_BLOCK)
    suffix = same_head & (r[:, None] > r[None, :])
    total = (r[:, None] // Q_BLOCK) == (jnp.arange(LANES)[None, :] // C_HEAD_DIM)
    return jnp.concatenate([suffix, total], axis=1).astype(BF16)


def _mixer_out_kernel(x_ref, att_ref, part_ref, g2_ref, mod_ref, wc_ref, wo_ref, n2g_ref,
                      w1_ref, w2_ref, fg_ref, o_ref, *, ff_chunk, final):
    mod = mod_ref[0]
    gate1 = mod[:, 2 * D_MODEL:3 * D_MODEL]
    shift2 = mod[:, 3 * D_MODEL:4 * D_MODEL]
    scale2 = mod[:, 4 * D_MODEL:5 * D_MODEL]
    gate2 = mod[:, 5 * D_MODEL:6 * D_MODEL]
    att = jnp.concatenate([att_ref[0, p] for p in range(HEAD_PAIRS)], axis=1)
    merged = part_ref[0] + g2_ref[0] * _dot(att, wc_ref[...])
    x1 = x_ref[0] + (1.0 + gate1) * _dot(merged.astype(BF16), wo_ref[...])
    hb = (_rms_norm(x1, n2g_ref[...]) * (1.0 + scale2) + shift2).astype(BF16)
    y = None
    for c in range(D_FF // ff_chunk):
        cols = slice(c * ff_chunk, (c + 1) * ff_chunk)
        hid = jnp.square(jnp.maximum(_dot(hb, w1_ref[:, cols]), 0.0)).astype(BF16)
        part = _dot(hid, w2_ref[cols, :])
        y = part if y is None else y + part
    x2 = x1 + (1.0 + gate2) * y
    if final:
        x2 = _rms_norm(x2, fg_ref[...])
    o_ref[0] = x2


def _mixer_out_call(x, att, part, g2, mod, wc, wo, n2g, w1, w2, fg, *, tm, final):
    bsz, seq, _ = x.shape
    const2 = lambda b, s: (0, 0)
    tok = lambda b, s: (b, s, 0)
    return pl.pallas_call(
        functools.partial(_mixer_out_kernel, ff_chunk=D_MODEL, final=final),
        grid=(bsz, seq // tm),
        in_specs=[
            pl.BlockSpec((1, tm, D_MODEL), tok),
            pl.BlockSpec((1, HEAD_PAIRS, tm, LANES), lambda b, s: (b, 0, s, 0)),
            pl.BlockSpec((1, tm, D_MODEL), tok),
            pl.BlockSpec((1, tm, D_MODEL), tok),
            pl.BlockSpec((1, 1, N_MOD * D_MODEL), lambda b, s: (b, 0, 0)),
            _resident((C_WIDTH, D_MODEL), const2),
            _resident((D_MODEL, D_MODEL), const2),
            _resident((1, D_MODEL), const2),
            _resident((D_MODEL, D_FF), const2),
            _resident((D_FF, D_MODEL), const2),
            _resident((1, D_MODEL), const2),
        ],
        out_specs=pl.BlockSpec((1, tm, D_MODEL), tok),
        out_shape=jax.ShapeDtypeStruct((bsz, seq, D_MODEL), F32),
        compiler_params=pltpu.CompilerParams(
            dimension_semantics=("arbitrary", "arbitrary"),
            vmem_limit_bytes=VMEM_LIMIT_BYTES),
        name="mixer_out",
    )(x, att, part, g2, mod, wc, wo, n2g, w1, w2, fg)


def kernel(x, c, ada_w, ada_b, norm1_g, w_in, a_ln_g, a_ln_b, a_ws, a_bs, b_conv_w,
           w_a_out, w_b_out, w_c_out, w_o, norm2_g, w_ff1, w_ff2, final_g):
    depth = ada_w.shape[0]
    bsz = x.shape[0]
    tm = 512
    c_pad = jnp.zeros((SUBLANES, D_MODEL), F32).at[:bsz].set(c)
    mod_all = _ada_call(c_pad, ada_w, ada_b)
    scan_mat = _scan_matrix()
    fg = final_g.reshape(1, D_MODEL)
    for l in range(depth):
        mod = mod_all[l, :bsz].reshape(bsz, 1, N_MOD * D_MODEL)
        bsb = jnp.broadcast_to(a_bs[l][:, :, None], (A_HEADS, CHUNK, A_WIDTH // A_HEADS))
        q, k, v, part, g2 = _mixer_in_call(
            x, mod, norm1_g[l].reshape(1, D_MODEL), w_in[l].astype(BF16),
            a_ln_g[l].reshape(1, A_WIDTH), a_ln_b[l].reshape(1, A_WIDTH), a_ws[l], bsb,
            b_conv_w[l], w_a_out[l].astype(BF16), w_b_out[l].astype(BF16), tm=tm)
        att = _attn_call(q, k, v, scan_mat)
        x = _mixer_out_call(
            x, att, part, g2, mod, w_c_out[l].astype(BF16), w_o[l].astype(BF16),
            norm2_g[l].reshape(1, D_MODEL), w_ff1[l].astype(BF16), w_ff2[l].astype(BF16),
            fg, tm=tm, final=(l == depth - 1))
    return x
```

```python
import functools

import jax
import jax.numpy as jnp
from jax import lax
from jax.experimental import pallas as pl
from jax.experimental.pallas import tpu as pltpu

D_MODEL = 1024
N_MOD = 6
A_WIDTH = 512
A_HEADS = 4
CHUNK = 128
B_WIDTH = 512
CONV_W = 3
C_WIDTH = 512
C_HEAD_DIM = 64
Q_BLOCK = 128
D_FF = 4 * D_MODEL
EPS = 1e-6

LANES = 128
SUBLANES = 8
HEAD_PAIRS = C_WIDTH // LANES
VMEM_LIMIT_BYTES = 56 * 1024 * 1024
F32_EXP_UNDERFLOW = 104.0
BEFORE_SEQUENCE_START = 1e30

_OFF_AU = 0
_OFF_AV = _OFF_AU + A_WIDTH
_OFF_BB = _OFF_AV + A_WIDTH
_OFF_BC = _OFF_BB + B_WIDTH
_OFF_BX = _OFF_BC + B_WIDTH
_OFF_Q = _OFF_BX + B_WIDTH
_OFF_K = _OFF_Q + C_WIDTH
_OFF_V = _OFF_K + C_WIDTH
_OFF_G = _OFF_V + C_WIDTH

F32 = jnp.float32
BF16 = jnp.bfloat16


def _dot(a, b):
    return jnp.dot(a, b, preferred_element_type=F32)


def _rms_norm(x, g):
    return x * lax.rsqrt(jnp.mean(x * x, axis=-1, keepdims=True) + EPS) * g


def _resident(block_shape, index_map):
    return pl.BlockSpec(block_shape, index_map, pipeline_mode=pl.Buffered(1))


def _ada_kernel(c_ref, w_ref, b_ref, o_ref):
    c = c_ref[...]
    c_act = c * jax.nn.sigmoid(c)
    o_ref[0] = _dot(c_act.astype(BF16), w_ref[0].astype(BF16)) + b_ref[0]


def _ada_call(c_pad, ada_w, ada_b):
    depth = ada_w.shape[0]
    rows = c_pad.shape[0]
    return pl.pallas_call(
        _ada_kernel,
        grid=(depth, N_MOD),
        in_specs=[
            pl.BlockSpec((rows, D_MODEL), lambda l, j: (0, 0)),
            pl.BlockSpec((1, D_MODEL, D_MODEL), lambda l, j: (l, 0, j)),
            pl.BlockSpec((1, 1, D_MODEL), lambda l, j: (l, 0, j)),
        ],
        out_specs=pl.BlockSpec((1, rows, D_MODEL), lambda l, j: (l, 0, j)),
        out_shape=jax.ShapeDtypeStruct((depth, rows, N_MOD * D_MODEL), F32),
        compiler_params=pltpu.CompilerParams(
            dimension_semantics=("arbitrary", "arbitrary"),
            vmem_limit_bytes=VMEM_LIMIT_BYTES),
        name="adaln",
    )(c_pad, ada_w, ada_b.reshape(depth, 1, N_MOD * D_MODEL))


def _mixer_in_kernel(x_ref, mod_ref, n1g_ref, w_in_ref, lng_ref, lnb_ref, ws_ref, bsb_ref,
                     conv_ref, wa_ref, wb_ref,
                     q_ref, k_ref, v_ref, part_ref, g2_ref,
                     ya_ref, halo_ref, *, tm):
    s_blk = pl.program_id(1)
    x = x_ref[0]
    mod = mod_ref[0]
    shift1 = mod[:, 0:D_MODEL]
    scale1 = mod[:, D_MODEL:2 * D_MODEL]
    h = _rms_norm(x, n1g_ref[...]) * (1.0 + scale1) + shift1
    hb = h.astype(BF16)

    def proj(off, width):
        return _dot(hb, w_in_ref[:, off:off + width])

    u = jax.nn.gelu(proj(_OFF_AU, A_WIDTH))
    v = jax.nn.gelu(proj(_OFF_AV, A_WIDTH))
    mu = jnp.mean(v, axis=-1, keepdims=True)
    var = jnp.mean(jnp.square(v - mu), axis=-1, keepdims=True)
    v = ((v - mu) * lax.rsqrt(var + EPS) * lng_ref[...] + lnb_ref[...]).astype(BF16)
    t_idx = lax.broadcasted_iota(jnp.int32, (CHUNK, CHUNK), 0)
    s_idx = lax.broadcasted_iota(jnp.int32, (CHUNK, CHUNK), 1)
    causal = s_idx <= t_idx
    group = A_WIDTH // A_HEADS
    for hd in range(A_HEADS):
        w_h = jnp.where(causal, ws_ref[hd], 0.0).astype(BF16)
        cols = slice(hd * group, (hd + 1) * group)
        for ck in range(tm // CHUNK):
            rows = slice(ck * CHUNK, (ck + 1) * CHUNK)
            mixed = _dot(w_h, v[rows, cols]) + bsb_ref[hd]
            ya_ref[rows, cols] = (u[rows, cols] * mixed).astype(BF16)
    y_a = _dot(ya_ref[...], wa_ref[...])

    b_b = proj(_OFF_BB, B_WIDTH)
    z = proj(_OFF_BC, B_WIDTH) * proj(_OFF_BX, B_WIDTH)

    @pl.when(s_blk == 0)
    def _():
        halo_ref[...] = jnp.zeros_like(halo_ref)

    prev1 = halo_ref[SUBLANES - 1:SUBLANES, :]
    prev2 = halo_ref[SUBLANES - 2:SUBLANES - 1, :]
    row = lax.broadcasted_iota(jnp.int32, (tm, B_WIDTH), 0)
    z1 = jnp.where(row == 0, prev1, pltpu.roll(z, 1, 0))
    z2 = jnp.where(row == 0, prev2, jnp.where(row == 1, prev1, pltpu.roll(z, 2, 0)))
    halo_ref[...] = z[tm - SUBLANES:, :]
    conv = conv_ref[0:1, :] * z2 + conv_ref[1:2, :] * z1 + conv_ref[2:3, :] * z
    y_b = _dot((b_b * conv).astype(BF16), wb_ref[...])

    q = proj(_OFF_Q, C_WIDTH) * (C_HEAD_DIM ** -0.5)
    k = proj(_OFF_K, C_WIDTH)
    vv = proj(_OFF_V, C_WIDTH)
    for p in range(HEAD_PAIRS):
        cols = slice(p * LANES, (p + 1) * LANES)
        q_ref[0, p] = q[:, cols].astype(BF16)
        k_ref[0, p] = k[:, cols].astype(BF16)
        v_ref[0, p] = vv[:, cols].astype(BF16)

    g0 = jax.nn.sigmoid(proj(_OFF_G, D_MODEL))
    g1 = jax.nn.sigmoid(proj(_OFF_G + D_MODEL, D_MODEL))
    part_ref[0] = g0 * y_a + g1 * y_b
    g2_ref[0] = jax.nn.sigmoid(proj(_OFF_G + 2 * D_MODEL, D_MODEL))


def _mixer_in_call(x, mod, n1g, w_in, lng, lnb, ws, bsb, conv_w, wa, wb, *, tm):
    bsz, seq, _ = x.shape
    in_cols = w_in.shape[1]
    const2 = lambda b, s: (0, 0)
    const3 = lambda b, s: (0, 0, 0)
    tok = lambda b, s: (b, s, 0)
    hp = lambda b, s: (b, 0, s, 0)
    qkv_shape = jax.ShapeDtypeStruct((bsz, HEAD_PAIRS, seq, LANES), BF16)
    return pl.pallas_call(
        functools.partial(_mixer_in_kernel, tm=tm),
        grid=(bsz, seq // tm),
        in_specs=[
            pl.BlockSpec((1, tm, D_MODEL), tok),
            pl.BlockSpec((1, 1, N_MOD * D_MODEL), lambda b, s: (b, 0, 0)),
            _resident((1, D_MODEL), const2),
            _resident((D_MODEL, in_cols), const2),
            _resident((1, A_WIDTH), const2),
            _resident((1, A_WIDTH), const2),
            _resident((A_HEADS, CHUNK, CHUNK), const3),
            _resident((A_HEADS, CHUNK, A_WIDTH // A_HEADS), const3),
            _resident((CONV_W, B_WIDTH), const2),
            _resident((A_WIDTH, D_MODEL), const2),
            _resident((B_WIDTH, D_MODEL), const2),
        ],
        out_specs=[
            pl.BlockSpec((1, HEAD_PAIRS, tm, LANES), hp),
            pl.BlockSpec((1, HEAD_PAIRS, tm, LANES), hp),
            pl.BlockSpec((1, HEAD_PAIRS, tm, LANES), hp),
            pl.BlockSpec((1, tm, D_MODEL), tok),
            pl.BlockSpec((1, tm, D_MODEL), tok),
        ],
        out_shape=[qkv_shape, qkv_shape, qkv_shape,
                   jax.ShapeDtypeStruct((bsz, seq, D_MODEL), F32),
                   jax.ShapeDtypeStruct((bsz, seq, D_MODEL), F32)],
        scratch_shapes=[pltpu.VMEM((tm, A_WIDTH), BF16),
                        pltpu.VMEM((SUBLANES, B_WIDTH), F32)],
        compiler_params=pltpu.CompilerParams(
            dimension_semantics=("arbitrary", "arbitrary"),
            vmem_limit_bytes=VMEM_LIMIT_BYTES),
        name="mixer_in",
    )(x, mod, n1g, w_in, lng, lnb, ws, bsb, conv_w, wa, wb)


PROLOGUE_BLOCKS = 3
Q_BLOCKS_PER_STEP = 4


def _attn_kernel(q_ref, k_ref, v_ref, scan_ref, keep_ref, o_ref, acc_ref, across_ref):
    first_head = lax.broadcasted_iota(jnp.int32, (Q_BLOCK, LANES), 1) < C_HEAD_DIM
    row = lax.broadcasted_iota(jnp.int32, (Q_BLOCK, 2 * Q_BLOCK), 0)
    col = lax.broadcasted_iota(jnp.int32, (Q_BLOCK, 2 * Q_BLOCK), 1) & (Q_BLOCK - 1)
    diag_mask = col < row
    sign_bit = jnp.uint32(0x80000000)

    def rows_of(j):
        return pl.ds(pl.multiple_of(j * Q_BLOCK, Q_BLOCK), Q_BLOCK)

    def head_stack(ref, p, j):
        blk = ref[0, p, rows_of(j), :]
        return jnp.concatenate([blk * keep_ref[0], blk * keep_ref[1]], axis=0)

    def visit(qi, j, diag):
        zs, sps = [], []
        for p in range(HEAD_PAIRS):
            z = lax.dot_general(q_ref[0, p, rows_of(qi), :], head_stack(k_ref, p, j),
                                (((1,), (1,)), ((), ())), preferred_element_type=F32)
            neg_abs = pltpu.bitcast(pltpu.bitcast(z, jnp.uint32) | sign_bit, F32)
            sp = jnp.maximum(z, 0.0) + jnp.log(1.0 + jnp.exp(neg_abs))
            zs.append(z)
            sps.append(jnp.where(diag_mask, sp, 0.0) if diag else sp)
        sp_all = jnp.concatenate(sps, axis=0)
        hi = sp_all.astype(BF16)
        lo = (sp_all - hi.astype(F32)).astype(BF16)
        incl = _dot(jnp.concatenate([hi, lo], axis=1), scan_ref[...])
        outs = []
        for p in range(HEAD_PAIRS):
            a = jnp.exp(zs[p] - incl[p * Q_BLOCK:(p + 1) * Q_BLOCK])
            if diag:
                a = jnp.where(diag_mask, a, 0.0)
            tot = jnp.where(first_head,
                            jnp.sum(sps[p][:, :Q_BLOCK], axis=-1, keepdims=True),
                            jnp.sum(sps[p][:, Q_BLOCK:], axis=-1, keepdims=True))
            outs.append((_dot(a.astype(BF16), head_stack(v_ref, p, j)), tot))
        return outs

    def all_underflowed(across):
        lowest = functools.reduce(jnp.minimum, across)
        return (jnp.min(lowest) > F32_EXP_UNDERFLOW).astype(jnp.int32)

    def query_block(qi, carry):
        i = pl.program_id(1) * Q_BLOCKS_PER_STEP + qi
        visits = [visit(qi, i, True)] + [visit(qi, jnp.maximum(i - d, 0), False)
                                         for d in range(1, PROLOGUE_BLOCKS)]
        across_all = []
        for p in range(HEAD_PAIRS):
            acc, across = visits[0][p]
            for d in range(1, PROLOGUE_BLOCKS):
                av, tot = visits[d][p]
                across = across + jnp.where(i >= d, 0.0, BEFORE_SEQUENCE_START)
                acc = acc + jnp.exp(-across) * av
                across = across + tot
            acc_ref[p] = acc
            across_ref[p] = across
            across_all.append(across)

        def body(c):
            j, _ = c
            across_all = []
            for p, (av, tot) in enumerate(visit(qi, j, False)):
                across = across_ref[p]
                acc_ref[p] += jnp.exp(-across) * av
                across_ref[p] = across + tot
                across_all.append(across + tot)
            return j - 1, all_underflowed(across_all)

        lax.while_loop(lambda c: (c[0] >= 0) & (c[1] == 0), body,
                       (i - PROLOGUE_BLOCKS, all_underflowed(across_all)))
        for p in range(HEAD_PAIRS):
            o_ref[0, p, rows_of(qi), :] = acc_ref[p].astype(BF16)
        return carry

    lax.fori_loop(0, Q_BLOCKS_PER_STEP, query_block, 0)


def _attn_call(q, k, v, scan_mat, head_keep):
    bsz, _, seq, _ = q.shape
    rows = Q_BLOCKS_PER_STEP * Q_BLOCK
    blk = lambda b, i: (b, 0, i, 0)
    whole = lambda b, i: (b, 0, 0, 0)
    return pl.pallas_call(
        _attn_kernel,
        grid=(bsz, seq // rows),
        in_specs=[
            pl.BlockSpec((1, HEAD_PAIRS, rows, LANES), blk),
            _resident((1, HEAD_PAIRS, seq, LANES), whole),
            _resident((1, HEAD_PAIRS, seq, LANES), whole),
            _resident(scan_mat.shape, lambda b, i: (0, 0)),
            _resident(head_keep.shape, lambda b, i: (0, 0, 0)),
        ],
        out_specs=pl.BlockSpec((1, HEAD_PAIRS, rows, LANES), blk),
        out_shape=jax.ShapeDtypeStruct(q.shape, BF16),
        scratch_shapes=[pltpu.VMEM((HEAD_PAIRS, Q_BLOCK, LANES), F32),
                        pltpu.VMEM((HEAD_PAIRS, Q_BLOCK, LANES), F32)],
        compiler_params=pltpu.CompilerParams(
            dimension_semantics=("arbitrary", "arbitrary"),
            vmem_limit_bytes=VMEM_LIMIT_BYTES),
        name="attention",
    )(q, k, v, scan_mat, head_keep)


def _scan_matrix():
    r = jnp.arange(2 * Q_BLOCK)
    same_head = (r[:, None] // Q_BLOCK) == (r[None, :] // Q_BLOCK)
    suffix = (same_head & (r[:, None] >= r[None, :])).astype(BF16)
    return jnp.concatenate([suffix, suffix], axis=0)


def _head_keep():
    first = (jnp.arange(LANES) < C_HEAD_DIM).astype(BF16)
    return jnp.broadcast_to(jnp.stack([first, 1 - first])[:, None, :], (2, Q_BLOCK, LANES))


def _mixer_out_kernel(x_ref, att_ref, part_ref, g2_ref, mod_ref, wc_ref, wo_ref, n2g_ref,
                      w1_ref, w2_ref, fg_ref, o_ref, *, ff_chunk, final):
    mod = mod_ref[0]
    gate1 = mod[:, 2 * D_MODEL:3 * D_MODEL]
    shift2 = mod[:, 3 * D_MODEL:4 * D_MODEL]
    scale2 = mod[:, 4 * D_MODEL:5 * D_MODEL]
    gate2 = mod[:, 5 * D_MODEL:6 * D_MODEL]
    att = jnp.concatenate([att_ref[0, p] for p in range(HEAD_PAIRS)], axis=1)
    merged = part_ref[0] + g2_ref[0] * _dot(att, wc_ref[...])
    x1 = x_ref[0] + (1.0 + gate1) * _dot(merged.astype(BF16), wo_ref[...])
    hb = (_rms_norm(x1, n2g_ref[...]) * (1.0 + scale2) + shift2).astype(BF16)
    y = None
    for c in range(D_FF // ff_chunk):
        cols = slice(c * ff_chunk, (c + 1) * ff_chunk)
        hid = jnp.square(jnp.maximum(_dot(hb, w1_ref[:, cols]), 0.0)).astype(BF16)
        part = _dot(hid, w2_ref[cols, :])
        y = part if y is None else y + part
    x2 = x1 + (1.0 + gate2) * y
    if final:
        x2 = _rms_norm(x2, fg_ref[...])
    o_ref[0] = x2


def _mixer_out_call(x, att, part, g2, mod, wc, wo, n2g, w1, w2, fg, *, tm, final):
    bsz, seq, _ = x.shape
    const2 = lambda b, s: (0, 0)
    tok = lambda b, s: (b, s, 0)
    return pl.pallas_call(
        functools.partial(_mixer_out_kernel, ff_chunk=D_MODEL, final=final),
        grid=(bsz, seq // tm),
        in_specs=[
            pl.BlockSpec((1, tm, D_MODEL), tok),
            pl.BlockSpec((1, HEAD_PAIRS, tm, LANES), lambda b, s: (b, 0, s, 0)),
            pl.BlockSpec((1, tm, D_MODEL), tok),
            pl.BlockSpec((1, tm, D_MODEL), tok),
            pl.BlockSpec((1, 1, N_MOD * D_MODEL), lambda b, s: (b, 0, 0)),
            _resident((C_WIDTH, D_MODEL), const2),
            _resident((D_MODEL, D_MODEL), const2),
            _resident((1, D_MODEL), const2),
            _resident((D_MODEL, D_FF), const2),
            _resident((D_FF, D_MODEL), const2),
            _resident((1, D_MODEL), const2),
        ],
        out_specs=pl.BlockSpec((1, tm, D_MODEL), tok),
        out_shape=jax.ShapeDtypeStruct((bsz, seq, D_MODEL), F32),
        compiler_params=pltpu.CompilerParams(
            dimension_semantics=("arbitrary", "arbitrary"),
            vmem_limit_bytes=VMEM_LIMIT_BYTES),
        name="mixer_out",
    )(x, att, part, g2, mod, wc, wo, n2g, w1, w2, fg)


def kernel(x, c, ada_w, ada_b, norm1_g, w_in, a_ln_g, a_ln_b, a_ws, a_bs, b_conv_w,
           w_a_out, w_b_out, w_c_out, w_o, norm2_g, w_ff1, w_ff2, final_g):
    depth = ada_w.shape[0]
    bsz = x.shape[0]
    tm = 512
    c_pad = jnp.zeros((SUBLANES, D_MODEL), F32).at[:bsz].set(c)
    mod_all = _ada_call(c_pad, ada_w, ada_b)
    scan_mat = _scan_matrix()
    head_keep = _head_keep()
    fg = final_g.reshape(1, D_MODEL)
    for l in range(depth):
        mod = mod_all[l, :bsz].reshape(bsz, 1, N_MOD * D_MODEL)
        bsb = jnp.broadcast_to(a_bs[l][:, :, None], (A_HEADS, CHUNK, A_WIDTH // A_HEADS))
        q, k, v, part, g2 = _mixer_in_call(
            x, mod, norm1_g[l].reshape(1, D_MODEL), w_in[l].astype(BF16),
            a_ln_g[l].reshape(1, A_WIDTH), a_ln_b[l].reshape(1, A_WIDTH), a_ws[l], bsb,
            b_conv_w[l], w_a_out[l].astype(BF16), w_b_out[l].astype(BF16), tm=tm)
        att = _attn_call(q, k, v, scan_mat, head_keep)
        x = _mixer_out_call(
            x, att, part, g2, mod, w_c_out[l].astype(BF16), w_o[l].astype(BF16),
            norm2_g[l].reshape(1, D_MODEL), w_ff1[l].astype(BF16), w_ff2[l].astype(BF16),
            fg, tm=tm, final=(l == depth - 1))
    return x
```

```python
import functools

import jax
import jax.numpy as jnp
from jax import lax
from jax.experimental import pallas as pl
from jax.experimental.pallas import tpu as pltpu

D_MODEL = 1024
N_MOD = 6
A_WIDTH = 512
A_HEADS = 4
CHUNK = 128
B_WIDTH = 512
CONV_W = 3
C_WIDTH = 512
C_HEAD_DIM = 64
Q_BLOCK = 128
D_FF = 4 * D_MODEL
EPS = 1e-6

LANES = 128
SUBLANES = 8
HEAD_PAIRS = C_WIDTH // LANES
VMEM_LIMIT_BYTES = 56 * 1024 * 1024
F32_EXP_UNDERFLOW = 104.0
BEFORE_SEQUENCE_START = 1e30

_OFF_AU = 0
_OFF_AV = _OFF_AU + A_WIDTH
_OFF_BB = _OFF_AV + A_WIDTH
_OFF_BC = _OFF_BB + B_WIDTH
_OFF_BX = _OFF_BC + B_WIDTH
_OFF_Q = _OFF_BX + B_WIDTH
_OFF_K = _OFF_Q + C_WIDTH
_OFF_V = _OFF_K + C_WIDTH
_OFF_G = _OFF_V + C_WIDTH

F32 = jnp.float32
BF16 = jnp.bfloat16


def _dot(a, b):
    return jnp.dot(a, b, preferred_element_type=F32)


def _rms_norm(x, g):
    return x * lax.rsqrt(jnp.mean(x * x, axis=-1, keepdims=True) + EPS) * g


def _resident(block_shape, index_map):
    return pl.BlockSpec(block_shape, index_map, pipeline_mode=pl.Buffered(1))


def _ada_kernel(c_ref, w_ref, b_ref, o_ref):
    c = c_ref[...]
    c_act = c * jax.nn.sigmoid(c)
    o_ref[0] = _dot(c_act.astype(BF16), w_ref[0].astype(BF16)) + b_ref[0]


def _ada_call(c_pad, ada_w, ada_b):
    depth = ada_w.shape[0]
    rows = c_pad.shape[0]
    return pl.pallas_call(
        _ada_kernel,
        grid=(depth, N_MOD),
        in_specs=[
            pl.BlockSpec((rows, D_MODEL), lambda l, j: (0, 0)),
            pl.BlockSpec((1, D_MODEL, D_MODEL), lambda l, j: (l, 0, j)),
            pl.BlockSpec((1, 1, D_MODEL), lambda l, j: (l, 0, j)),
        ],
        out_specs=pl.BlockSpec((1, rows, D_MODEL), lambda l, j: (l, 0, j)),
        out_shape=jax.ShapeDtypeStruct((depth, rows, N_MOD * D_MODEL), F32),
        compiler_params=pltpu.CompilerParams(
            dimension_semantics=("arbitrary", "arbitrary"),
            vmem_limit_bytes=VMEM_LIMIT_BYTES),
        name="adaln",
    )(c_pad, ada_w, ada_b.reshape(depth, 1, N_MOD * D_MODEL))


def _mixer_in_kernel(x_ref, mod_ref, n1g_ref, w_in_ref, lng_ref, lnb_ref, ws_ref, bsb_ref,
                     conv_ref, wa_ref, wb_ref,
                     q_ref, k_ref, v_ref, part_ref, g2_ref,
                     ya_ref, halo_ref, *, tm):
    s_blk = pl.program_id(1)
    x = x_ref[0]
    mod = mod_ref[0]
    shift1 = mod[:, 0:D_MODEL]
    scale1 = mod[:, D_MODEL:2 * D_MODEL]
    h = _rms_norm(x, n1g_ref[0]) * (1.0 + scale1) + shift1
    hb = h.astype(BF16)

    def proj(off, width):
        return _dot(hb, w_in_ref[0, :, off:off + width])

    a_u = proj(_OFF_AU, A_WIDTH)
    a_v = proj(_OFF_AV, A_WIDTH)
    b_b = proj(_OFF_BB, B_WIDTH)
    z = proj(_OFF_BC, B_WIDTH) * proj(_OFF_BX, B_WIDTH)

    u = jax.nn.gelu(a_u)
    v = jax.nn.gelu(a_v)
    mu = jnp.mean(v, axis=-1, keepdims=True)
    var = jnp.mean(jnp.square(v - mu), axis=-1, keepdims=True)
    v = ((v - mu) * lax.rsqrt(var + EPS) * lng_ref[0] + lnb_ref[0]).astype(BF16)
    t_idx = lax.broadcasted_iota(jnp.int32, (CHUNK, CHUNK), 0)
    s_idx = lax.broadcasted_iota(jnp.int32, (CHUNK, CHUNK), 1)
    causal = s_idx <= t_idx
    group = A_WIDTH // A_HEADS
    for hd in range(A_HEADS):
        w_h = jnp.where(causal, ws_ref[0, hd], 0.0).astype(BF16)
        cols = slice(hd * group, (hd + 1) * group)
        for ck in range(tm // CHUNK):
            rows = slice(ck * CHUNK, (ck + 1) * CHUNK)
            mixed = _dot(w_h, v[rows, cols]) + bsb_ref[0, hd]
            ya_ref[rows, cols] = (u[rows, cols] * mixed).astype(BF16)
    y_a = _dot(ya_ref[...], wa_ref[0])

    @pl.when(s_blk == 0)
    def _():
        halo_ref[...] = jnp.zeros_like(halo_ref)

    prev1 = halo_ref[SUBLANES - 1:SUBLANES, :]
    prev2 = halo_ref[SUBLANES - 2:SUBLANES - 1, :]
    row = lax.broadcasted_iota(jnp.int32, (tm, B_WIDTH), 0)
    z1 = jnp.where(row == 0, prev1, pltpu.roll(z, 1, 0))
    z2 = jnp.where(row == 0, prev2, jnp.where(row == 1, prev1, pltpu.roll(z, 2, 0)))
    halo_ref[...] = z[tm - SUBLANES:, :]
    conv = conv_ref[0, 0:1, :] * z2 + conv_ref[0, 1:2, :] * z1 + conv_ref[0, 2:3, :] * z
    y_b = _dot((b_b * conv).astype(BF16), wb_ref[0])

    q = proj(_OFF_Q, C_WIDTH) * (C_HEAD_DIM ** -0.5)
    k = proj(_OFF_K, C_WIDTH)
    vv = proj(_OFF_V, C_WIDTH)
    for p in range(HEAD_PAIRS):
        cols = slice(p * LANES, (p + 1) * LANES)
        q_ref[0, p] = q[:, cols].astype(BF16)
        k_ref[0, p] = k[:, cols].astype(BF16)
        v_ref[0, p] = vv[:, cols].astype(BF16)

    g0 = jax.nn.sigmoid(proj(_OFF_G, D_MODEL))
    g1 = jax.nn.sigmoid(proj(_OFF_G + D_MODEL, D_MODEL))
    part_ref[0] = g0 * y_a + g1 * y_b
    g2_ref[0] = jax.nn.sigmoid(proj(_OFF_G + 2 * D_MODEL, D_MODEL))


def _mixer_in_call(layer, x, mod, n1g, w_in, lng, lnb, ws, bsb, conv_w, wa, wb, *, tm):
    bsz, seq, _ = x.shape
    in_cols = w_in.shape[-1]
    lay3 = lambda b, s: (layer, 0, 0)
    lay4 = lambda b, s: (layer, 0, 0, 0)
    tok = lambda b, s: (b, s, 0)
    hp = lambda b, s: (b, 0, s, 0)
    qkv_shape = jax.ShapeDtypeStruct((bsz, HEAD_PAIRS, seq, LANES), BF16)
    return pl.pallas_call(
        functools.partial(_mixer_in_kernel, tm=tm),
        grid=(bsz, seq // tm),
        in_specs=[
            pl.BlockSpec((1, tm, D_MODEL), tok),
            pl.BlockSpec((1, 1, N_MOD * D_MODEL), lambda b, s: (layer * SUBLANES + b, 0, 0)),
            _resident((1, 1, D_MODEL), lay3),
            _resident((1, D_MODEL, in_cols), lay3),
            _resident((1, 1, A_WIDTH), lay3),
            _resident((1, 1, A_WIDTH), lay3),
            _resident((1, A_HEADS, CHUNK, CHUNK), lay4),
            _resident((1, A_HEADS, CHUNK, A_WIDTH // A_HEADS), lay4),
            _resident((1, CONV_W, B_WIDTH), lay3),
            _resident((1, A_WIDTH, D_MODEL), lay3),
            _resident((1, B_WIDTH, D_MODEL), lay3),
        ],
        out_specs=[
            pl.BlockSpec((1, HEAD_PAIRS, tm, LANES), hp),
            pl.BlockSpec((1, HEAD_PAIRS, tm, LANES), hp),
            pl.BlockSpec((1, HEAD_PAIRS, tm, LANES), hp),
            pl.BlockSpec((1, tm, D_MODEL), tok),
            pl.BlockSpec((1, tm, D_MODEL), tok),
        ],
        out_shape=[qkv_shape, qkv_shape, qkv_shape,
                   jax.ShapeDtypeStruct((bsz, seq, D_MODEL), F32),
                   jax.ShapeDtypeStruct((bsz, seq, D_MODEL), F32)],
        scratch_shapes=[pltpu.VMEM((tm, A_WIDTH), BF16),
                        pltpu.VMEM((SUBLANES, B_WIDTH), F32)],
        compiler_params=pltpu.CompilerParams(
            dimension_semantics=("arbitrary", "arbitrary"),
            vmem_limit_bytes=VMEM_LIMIT_BYTES),
        name="mixer_in",
    )(x, mod, n1g, w_in, lng, lnb, ws, bsb, conv_w, wa, wb)


PROLOGUE_BLOCKS = 3
Q_BLOCKS_PER_STEP = 4


def _attn_kernel(q_ref, k_ref, v_ref, scan_ref, keep_ref, o_ref, acc_ref, across_ref):
    first_head = lax.broadcasted_iota(jnp.int32, (Q_BLOCK, LANES), 1) < C_HEAD_DIM
    row = lax.broadcasted_iota(jnp.int32, (Q_BLOCK, 2 * Q_BLOCK), 0)
    col = lax.broadcasted_iota(jnp.int32, (Q_BLOCK, 2 * Q_BLOCK), 1) & (Q_BLOCK - 1)
    diag_mask = col < row

    def rows_of(j):
        return pl.ds(pl.multiple_of(j * Q_BLOCK, Q_BLOCK), Q_BLOCK)

    def head_stack(ref, p, j):
        blk = ref[0, p, rows_of(j), :]
        return jnp.concatenate([blk * keep_ref[0], blk * keep_ref[1]], axis=0)

    def visit(qi, j, diag):
        zs, sps = [], []
        for p in range(HEAD_PAIRS):
            z = lax.dot_general(q_ref[0, p, rows_of(qi), :], head_stack(k_ref, p, j),
                                (((1,), (1,)), ((), ())), preferred_element_type=F32)
            sp = jnp.maximum(z, 0.0) + jnp.log(1.0 + jnp.exp(-jnp.abs(z)))
            zs.append(z)
            sps.append(jnp.where(diag_mask, sp, 0.0) if diag else sp)
        sp_all = jnp.concatenate(sps, axis=0)
        hi = sp_all.astype(BF16)
        lo = (sp_all - hi.astype(F32)).astype(BF16)
        incl = _dot(jnp.concatenate([hi, lo], axis=1), scan_ref[...])
        outs = []
        for p in range(HEAD_PAIRS):
            a = jnp.exp(zs[p] - incl[p * Q_BLOCK:(p + 1) * Q_BLOCK])
            if diag:
                a = jnp.where(diag_mask, a, 0.0)
            tot = jnp.where(first_head,
                            jnp.sum(sps[p][:, :Q_BLOCK], axis=-1, keepdims=True),
                            jnp.sum(sps[p][:, Q_BLOCK:], axis=-1, keepdims=True))
            outs.append((_dot(a.astype(BF16), head_stack(v_ref, p, j)), tot))
        return outs

    def all_underflowed(across):
        lowest = functools.reduce(jnp.minimum, across)
        return (jnp.min(lowest) > F32_EXP_UNDERFLOW).astype(jnp.int32)

    def query_block(qi, carry):
        i = pl.program_id(1) * Q_BLOCKS_PER_STEP + qi
        visits = [visit(qi, i, True)] + [visit(qi, jnp.maximum(i - d, 0), False)
                                         for d in range(1, PROLOGUE_BLOCKS)]
        across_all = []
        for p in range(HEAD_PAIRS):
            acc, across = visits[0][p]
            for d in range(1, PROLOGUE_BLOCKS):
                av, tot = visits[d][p]
                across = across + jnp.where(i >= d, 0.0, BEFORE_SEQUENCE_START)
                acc = acc + jnp.exp(-across) * av
                across = across + tot
            acc_ref[p] = acc
            across_ref[p] = across
            across_all.append(across)

        def body(c):
            j, _ = c
            across_all = []
            for p, (av, tot) in enumerate(visit(qi, j, False)):
                across = across_ref[p]
                acc_ref[p] += jnp.exp(-across) * av
                across_ref[p] = across + tot
                across_all.append(across + tot)
            return j - 1, all_underflowed(across_all)

        lax.while_loop(lambda c: (c[0] >= 0) & (c[1] == 0), body,
                       (i - PROLOGUE_BLOCKS, all_underflowed(across_all)))
        for p in range(HEAD_PAIRS):
            o_ref[0, p, rows_of(qi), :] = acc_ref[p].astype(BF16)
        return carry

    lax.fori_loop(0, Q_BLOCKS_PER_STEP, query_block, 0)


def _attn_call(q, k, v, scan_mat, head_keep):
    bsz, _, seq, _ = q.shape
    rows = Q_BLOCKS_PER_STEP * Q_BLOCK
    blk = lambda b, i: (b, 0, i, 0)
    whole = lambda b, i: (b, 0, 0, 0)
    return pl.pallas_call(
        _attn_kernel,
        grid=(bsz, seq // rows),
        in_specs=[
            pl.BlockSpec((1, HEAD_PAIRS, rows, LANES), blk),
            _resident((1, HEAD_PAIRS, seq, LANES), whole),
            _resident((1, HEAD_PAIRS, seq, LANES), whole),
            _resident(scan_mat.shape, lambda b, i: (0, 0)),
            _resident(head_keep.shape, lambda b, i: (0, 0, 0)),
        ],
        out_specs=pl.BlockSpec((1, HEAD_PAIRS, rows, LANES), blk),
        out_shape=jax.ShapeDtypeStruct(q.shape, BF16),
        scratch_shapes=[pltpu.VMEM((HEAD_PAIRS, Q_BLOCK, LANES), F32),
                        pltpu.VMEM((HEAD_PAIRS, Q_BLOCK, LANES), F32)],
        compiler_params=pltpu.CompilerParams(
            dimension_semantics=("arbitrary", "arbitrary"),
            vmem_limit_bytes=VMEM_LIMIT_BYTES),
        name="attention",
    )(q, k, v, scan_mat, head_keep)


def _scan_matrix():
    r = jnp.arange(2 * Q_BLOCK)
    same_head = (r[:, None] // Q_BLOCK) == (r[None, :] // Q_BLOCK)
    suffix = (same_head & (r[:, None] >= r[None, :])).astype(BF16)
    return jnp.concatenate([suffix, suffix], axis=0)


def _head_keep():
    first = (jnp.arange(LANES) < C_HEAD_DIM).astype(BF16)
    return jnp.broadcast_to(jnp.stack([first, 1 - first])[:, None, :], (2, Q_BLOCK, LANES))


def _mixer_out_kernel(x_ref, att_ref, part_ref, g2_ref, mod_ref, wc_ref, wo_ref, n2g_ref,
                      w1_ref, w2_ref, fg_ref, o_ref, *, ff_chunk, final):
    mod = mod_ref[0]
    gate1 = mod[:, 2 * D_MODEL:3 * D_MODEL]
    shift2 = mod[:, 3 * D_MODEL:4 * D_MODEL]
    scale2 = mod[:, 4 * D_MODEL:5 * D_MODEL]
    gate2 = mod[:, 5 * D_MODEL:6 * D_MODEL]
    att = jnp.concatenate([att_ref[0, p] for p in range(HEAD_PAIRS)], axis=1)
    merged = part_ref[0] + g2_ref[0] * _dot(att, wc_ref[0])
    x1 = x_ref[0] + (1.0 + gate1) * _dot(merged.astype(BF16), wo_ref[0])
    hb = (_rms_norm(x1, n2g_ref[0]) * (1.0 + scale2) + shift2).astype(BF16)
    y = None
    for c in range(D_FF // ff_chunk):
        cols = slice(c * ff_chunk, (c + 1) * ff_chunk)
        hid = jnp.square(jnp.maximum(_dot(hb, w1_ref[0, :, cols]), 0.0)).astype(BF16)
        part = _dot(hid, w2_ref[0, cols, :])
        y = part if y is None else y + part
    x2 = x1 + (1.0 + gate2) * y
    if final:
        x2 = _rms_norm(x2, fg_ref[...])
    o_ref[0] = x2


def _mixer_out_call(layer, x, att, part, g2, mod, wc, wo, n2g, w1, w2, fg, *, tm, final):
    bsz, seq, _ = x.shape
    lay3 = lambda b, s: (layer, 0, 0)
    tok = lambda b, s: (b, s, 0)
    return pl.pallas_call(
        functools.partial(_mixer_out_kernel, ff_chunk=D_MODEL, final=final),
        grid=(bsz, seq // tm),
        in_specs=[
            pl.BlockSpec((1, tm, D_MODEL), tok),
            pl.BlockSpec((1, HEAD_PAIRS, tm, LANES), lambda b, s: (b, 0, s, 0)),
            pl.BlockSpec((1, tm, D_MODEL), tok),
            pl.BlockSpec((1, tm, D_MODEL), tok),
            pl.BlockSpec((1, 1, N_MOD * D_MODEL), lambda b, s: (layer * SUBLANES + b, 0, 0)),
            _resident((1, C_WIDTH, D_MODEL), lay3),
            _resident((1, D_MODEL, D_MODEL), lay3),
            _resident((1, 1, D_MODEL), lay3),
            _resident((1, D_MODEL, D_FF), lay3),
            _resident((1, D_FF, D_MODEL), lay3),
            _resident((1, D_MODEL), lambda b, s: (0, 0)),
        ],
        out_specs=pl.BlockSpec((1, tm, D_MODEL), tok),
        out_shape=jax.ShapeDtypeStruct((bsz, seq, D_MODEL), F32),
        compiler_params=pltpu.CompilerParams(
            dimension_semantics=("arbitrary", "arbitrary"),
            vmem_limit_bytes=VMEM_LIMIT_BYTES),
        name="mixer_out",
    )(x, att, part, g2, mod, wc, wo, n2g, w1, w2, fg)


def kernel(x, c, ada_w, ada_b, norm1_g, w_in, a_ln_g, a_ln_b, a_ws, a_bs, b_conv_w,
           w_a_out, w_b_out, w_c_out, w_o, norm2_g, w_ff1, w_ff2, final_g):
    depth = ada_w.shape[0]
    bsz = x.shape[0]
    tm = 512
    c_pad = jnp.zeros((SUBLANES, D_MODEL), F32).at[:bsz].set(c)
    mod_all = _ada_call(c_pad, ada_w, ada_b).reshape(depth * SUBLANES, 1, N_MOD * D_MODEL)
    scan_mat = _scan_matrix()
    head_keep = _head_keep()
    fg = final_g.reshape(1, D_MODEL)
    n1g = norm1_g.reshape(depth, 1, D_MODEL)
    n2g = norm2_g.reshape(depth, 1, D_MODEL)
    lng = a_ln_g.reshape(depth, 1, A_WIDTH)
    lnb = a_ln_b.reshape(depth, 1, A_WIDTH)
    bsb = jnp.broadcast_to(a_bs[:, :, :, None], (depth, A_HEADS, CHUNK, A_WIDTH // A_HEADS))
    w_in_b, wa, wb, wc, wo, w1, w2 = (w.astype(BF16) for w in
                                      (w_in, w_a_out, w_b_out, w_c_out, w_o, w_ff1, w_ff2))
    for l in range(depth):
        q, k, v, part, g2 = _mixer_in_call(l, x, mod_all, n1g, w_in_b, lng, lnb, a_ws, bsb,
                                           b_conv_w, wa, wb, tm=tm)
        att = _attn_call(q, k, v, scan_mat, head_keep)
        x = _mixer_out_call(l, x, att, part, g2, mod_all, wc, wo, n2g, w1, w2, fg,
                            tm=tm, final=(l == depth - 1))
    return x
```

```python
import functools

import jax
import jax.numpy as jnp
from jax import lax
from jax.experimental import pallas as pl
from jax.experimental.pallas import tpu as pltpu

D_MODEL = 1024
N_MOD = 6
A_WIDTH = 512
A_HEADS = 4
CHUNK = 128
B_WIDTH = 512
CONV_W = 3
C_WIDTH = 512
C_HEAD_DIM = 64
Q_BLOCK = 128
D_FF = 4 * D_MODEL
EPS = 1e-6

LANES = 128
SUBLANES = 8
HEAD_PAIRS = C_WIDTH // LANES
VMEM_LIMIT_BYTES = 56 * 1024 * 1024
F32_EXP_UNDERFLOW = 104.0
BEFORE_SEQUENCE_START = 1e30

_OFF_AU = 0
_OFF_AV = _OFF_AU + A_WIDTH
_OFF_BB = _OFF_AV + A_WIDTH
_OFF_BC = _OFF_BB + B_WIDTH
_OFF_BX = _OFF_BC + B_WIDTH
_OFF_Q = _OFF_BX + B_WIDTH
_OFF_K = _OFF_Q + C_WIDTH
_OFF_V = _OFF_K + C_WIDTH
_OFF_G = _OFF_V + C_WIDTH

F32 = jnp.float32
BF16 = jnp.bfloat16


def _dot(a, b):
    return jnp.dot(a, b, preferred_element_type=F32)


def _rms_norm(x, g):
    return x * lax.rsqrt(jnp.mean(x * x, axis=-1, keepdims=True) + EPS) * g


def _resident(block_shape, index_map):
    return pl.BlockSpec(block_shape, index_map, pipeline_mode=pl.Buffered(1))


def _ada_kernel(c_ref, w_ref, b_ref, o_ref):
    c = c_ref[...]
    c_act = c * jax.nn.sigmoid(c)
    o_ref[0] = _dot(c_act.astype(BF16), w_ref[0].astype(BF16)) + b_ref[0]


def _ada_call(c_pad, ada_w, ada_b):
    depth = ada_w.shape[0]
    rows = c_pad.shape[0]
    return pl.pallas_call(
        _ada_kernel,
        grid=(depth, N_MOD),
        in_specs=[
            pl.BlockSpec((rows, D_MODEL), lambda l, j: (0, 0)),
            pl.BlockSpec((1, D_MODEL, D_MODEL), lambda l, j: (l, 0, j)),
            pl.BlockSpec((1, 1, D_MODEL), lambda l, j: (l, 0, j)),
        ],
        out_specs=pl.BlockSpec((1, rows, D_MODEL), lambda l, j: (l, 0, j)),
        out_shape=jax.ShapeDtypeStruct((depth, rows, N_MOD * D_MODEL), F32),
        compiler_params=pltpu.CompilerParams(
            dimension_semantics=("arbitrary", "arbitrary"),
            vmem_limit_bytes=VMEM_LIMIT_BYTES),
        name="adaln",
    )(c_pad, ada_w, ada_b.reshape(depth, 1, N_MOD * D_MODEL))


def _mixer_in_kernel(x_ref, mod_ref, n1g_ref, w_in_ref, lng_ref, lnb_ref, ws_ref, bsb_ref,
                     conv_ref, wa_ref, wb_ref,
                     q_ref, k_ref, v_ref, part_ref, g2_ref,
                     ya_ref, halo_ref, *, tm):
    s_blk = pl.program_id(1)
    x = x_ref[0]
    mod = mod_ref[0]
    shift1 = mod[:, 0:D_MODEL]
    scale1 = mod[:, D_MODEL:2 * D_MODEL]
    h = _rms_norm(x, n1g_ref[0]) * (1.0 + scale1) + shift1
    hb = h.astype(BF16)

    def proj(off, width):
        return _dot(hb, w_in_ref[0, :, off:off + width])

    a_u = proj(_OFF_AU, A_WIDTH)
    a_v = proj(_OFF_AV, A_WIDTH)
    b_b = proj(_OFF_BB, B_WIDTH)
    z = proj(_OFF_BC, B_WIDTH) * proj(_OFF_BX, B_WIDTH)

    u = jax.nn.gelu(a_u)
    v = jax.nn.gelu(a_v)
    mu = jnp.mean(v, axis=-1, keepdims=True)
    var = jnp.mean(jnp.square(v - mu), axis=-1, keepdims=True)
    v = ((v - mu) * lax.rsqrt(var + EPS) * lng_ref[0] + lnb_ref[0]).astype(BF16)
    t_idx = lax.broadcasted_iota(jnp.int32, (CHUNK, CHUNK), 0)
    s_idx = lax.broadcasted_iota(jnp.int32, (CHUNK, CHUNK), 1)
    causal = s_idx <= t_idx
    group = A_WIDTH // A_HEADS
    for hd in range(A_HEADS):
        w_h = jnp.where(causal, ws_ref[0, hd], 0.0).astype(BF16)
        cols = slice(hd * group, (hd + 1) * group)
        for ck in range(tm // CHUNK):
            rows = slice(ck * CHUNK, (ck + 1) * CHUNK)
            mixed = _dot(w_h, v[rows, cols]) + bsb_ref[0, hd]
            ya_ref[rows, cols] = (u[rows, cols] * mixed).astype(BF16)
    y_a = _dot(ya_ref[...], wa_ref[0])

    @pl.when(s_blk == 0)
    def _():
        halo_ref[...] = jnp.zeros_like(halo_ref)

    prev1 = halo_ref[SUBLANES - 1:SUBLANES, :]
    prev2 = halo_ref[SUBLANES - 2:SUBLANES - 1, :]
    row = lax.broadcasted_iota(jnp.int32, (tm, B_WIDTH), 0)
    z1 = jnp.where(row == 0, prev1, pltpu.roll(z, 1, 0))
    z2 = jnp.where(row == 0, prev2, jnp.where(row == 1, prev1, pltpu.roll(z, 2, 0)))
    halo_ref[...] = z[tm - SUBLANES:, :]
    conv = conv_ref[0, 0:1, :] * z2 + conv_ref[0, 1:2, :] * z1 + conv_ref[0, 2:3, :] * z
    y_b = _dot((b_b * conv).astype(BF16), wb_ref[0])

    q = proj(_OFF_Q, C_WIDTH) * (C_HEAD_DIM ** -0.5)
    k = proj(_OFF_K, C_WIDTH)
    vv = proj(_OFF_V, C_WIDTH)
    for p in range(HEAD_PAIRS):
        cols = slice(p * LANES, (p + 1) * LANES)
        q_ref[0, p] = q[:, cols].astype(BF16)
        k_ref[0, p] = k[:, cols].astype(BF16)
        v_ref[0, p] = vv[:, cols].astype(BF16)

    g0 = jax.nn.sigmoid(proj(_OFF_G, D_MODEL))
    g1 = jax.nn.sigmoid(proj(_OFF_G + D_MODEL, D_MODEL))
    part_ref[0] = g0 * y_a + g1 * y_b
    g2_ref[0] = jax.nn.sigmoid(proj(_OFF_G + 2 * D_MODEL, D_MODEL))


def _mixer_in_call(layer, x, mod, n1g, w_in, lng, lnb, ws, bsb, conv_w, wa, wb, *, tm):
    bsz, seq, _ = x.shape
    in_cols = w_in.shape[-1]
    lay3 = lambda b, s: (layer, 0, 0)
    lay4 = lambda b, s: (layer, 0, 0, 0)
    tok = lambda b, s: (b, s, 0)
    hp = lambda b, s: (b, 0, s, 0)
    qkv_shape = jax.ShapeDtypeStruct((bsz, HEAD_PAIRS, seq, LANES), BF16)
    return pl.pallas_call(
        functools.partial(_mixer_in_kernel, tm=tm),
        grid=(bsz, seq // tm),
        in_specs=[
            pl.BlockSpec((1, tm, D_MODEL), tok),
            pl.BlockSpec((1, 1, N_MOD * D_MODEL), lambda b, s: (layer * SUBLANES + b, 0, 0)),
            _resident((1, 1, D_MODEL), lay3),
            _resident((1, D_MODEL, in_cols), lay3),
            _resident((1, 1, A_WIDTH), lay3),
            _resident((1, 1, A_WIDTH), lay3),
            _resident((1, A_HEADS, CHUNK, CHUNK), lay4),
            _resident((1, A_HEADS, CHUNK, A_WIDTH // A_HEADS), lay4),
            _resident((1, CONV_W, B_WIDTH), lay3),
            _resident((1, A_WIDTH, D_MODEL), lay3),
            _resident((1, B_WIDTH, D_MODEL), lay3),
        ],
        out_specs=[
            pl.BlockSpec((1, HEAD_PAIRS, tm, LANES), hp),
            pl.BlockSpec((1, HEAD_PAIRS, tm, LANES), hp),
            pl.BlockSpec((1, HEAD_PAIRS, tm, LANES), hp),
            pl.BlockSpec((1, tm, D_MODEL), tok),
            pl.BlockSpec((1, tm, D_MODEL), tok),
        ],
        out_shape=[qkv_shape, qkv_shape, qkv_shape,
                   jax.ShapeDtypeStruct((bsz, seq, D_MODEL), F32),
                   jax.ShapeDtypeStruct((bsz, seq, D_MODEL), F32)],
        scratch_shapes=[pltpu.VMEM((tm, A_WIDTH), BF16),
                        pltpu.VMEM((SUBLANES, B_WIDTH), F32)],
        compiler_params=pltpu.CompilerParams(
            dimension_semantics=("arbitrary", "arbitrary"),
            vmem_limit_bytes=VMEM_LIMIT_BYTES),
        name="mixer_in",
    )(x, mod, n1g, w_in, lng, lnb, ws, bsb, conv_w, wa, wb)


PROLOGUE_BLOCKS = 3
Q_BLOCKS_PER_STEP = 8


def _attn_kernel(q_ref, k_ref, v_ref, scan_ref, keep_ref, o_ref, acc_ref, across_ref):
    first_head = lax.broadcasted_iota(jnp.int32, (Q_BLOCK, LANES), 1) < C_HEAD_DIM
    row = lax.broadcasted_iota(jnp.int32, (Q_BLOCK, 2 * Q_BLOCK), 0)
    col = lax.broadcasted_iota(jnp.int32, (Q_BLOCK, 2 * Q_BLOCK), 1) & (Q_BLOCK - 1)
    diag_mask = col < row

    def rows_of(j):
        return pl.ds(pl.multiple_of(j * Q_BLOCK, Q_BLOCK), Q_BLOCK)

    def head_stack(ref, p, j):
        blk = ref[0, p, rows_of(j), :]
        return jnp.concatenate([blk * keep_ref[0], blk * keep_ref[1]], axis=0)

    def visit(qi, blocks):
        zs, sps = [], []
        for j, diag in blocks:
            for p in range(HEAD_PAIRS):
                z = lax.dot_general(q_ref[0, p, rows_of(qi), :], head_stack(k_ref, p, j),
                                    (((1,), (1,)), ((), ())), preferred_element_type=F32)
                sp = jnp.maximum(z, 0.0) + jnp.log(1.0 + jnp.exp(-jnp.abs(z)))
                zs.append(z)
                sps.append(jnp.where(diag_mask, sp, 0.0) if diag else sp)
        sp_all = jnp.concatenate(sps, axis=0)
        hi = sp_all.astype(BF16)
        lo = (sp_all - hi.astype(F32)).astype(BF16)
        incl = _dot(jnp.concatenate([hi, lo], axis=1), scan_ref[...])
        outs = []
        for b, (j, diag) in enumerate(blocks):
            per_pair = []
            for p in range(HEAD_PAIRS):
                n = b * HEAD_PAIRS + p
                a = jnp.exp(zs[n] - incl[n * Q_BLOCK:(n + 1) * Q_BLOCK])
                if diag:
                    a = jnp.where(diag_mask, a, 0.0)
                tot = jnp.where(first_head,
                                jnp.sum(sps[n][:, :Q_BLOCK], axis=-1, keepdims=True),
                                jnp.sum(sps[n][:, Q_BLOCK:], axis=-1, keepdims=True))
                per_pair.append((_dot(a.astype(BF16), head_stack(v_ref, p, j)), tot))
            outs.append(per_pair)
        return outs

    def all_underflowed(across):
        lowest = functools.reduce(jnp.minimum, across)
        return (jnp.min(lowest) > F32_EXP_UNDERFLOW).astype(jnp.int32)

    def query_block(qi, carry):
        i = pl.program_id(1) * Q_BLOCKS_PER_STEP + qi
        visits = visit(qi, [(i, True)] + [(jnp.maximum(i - d, 0), False)
                                          for d in range(1, PROLOGUE_BLOCKS)])
        across_all = []
        for p in range(HEAD_PAIRS):
            acc, across = visits[0][p]
            for d in range(1, PROLOGUE_BLOCKS):
                av, tot = visits[d][p]
                across = across + jnp.where(i >= d, 0.0, BEFORE_SEQUENCE_START)
                acc = acc + jnp.exp(-across) * av
                across = across + tot
            acc_ref[p] = acc
            across_ref[p] = across
            across_all.append(across)

        def body(c):
            j, _ = c
            across_all = []
            for p, (av, tot) in enumerate(visit(qi, [(j, False)])[0]):
                across = across_ref[p]
                acc_ref[p] += jnp.exp(-across) * av
                across_ref[p] = across + tot
                across_all.append(across + tot)
            return j - 1, all_underflowed(across_all)

        lax.while_loop(lambda c: (c[0] >= 0) & (c[1] == 0), body,
                       (i - PROLOGUE_BLOCKS, all_underflowed(across_all)))
        for p in range(HEAD_PAIRS):
            o_ref[0, p, rows_of(qi), :] = acc_ref[p].astype(BF16)
        return carry

    lax.fori_loop(0, Q_BLOCKS_PER_STEP, query_block, 0)


def _attn_call(q, k, v, scan_mat, head_keep):
    bsz, _, seq, _ = q.shape
    rows = Q_BLOCKS_PER_STEP * Q_BLOCK
    blk = lambda b, i: (b, 0, i, 0)
    whole = lambda b, i: (b, 0, 0, 0)
    return pl.pallas_call(
        _attn_kernel,
        grid=(bsz, seq // rows),
        in_specs=[
            pl.BlockSpec((1, HEAD_PAIRS, rows, LANES), blk),
            _resident((1, HEAD_PAIRS, seq, LANES), whole),
            _resident((1, HEAD_PAIRS, seq, LANES), whole),
            _resident(scan_mat.shape, lambda b, i: (0, 0)),
            _resident(head_keep.shape, lambda b, i: (0, 0, 0)),
        ],
        out_specs=pl.BlockSpec((1, HEAD_PAIRS, rows, LANES), blk),
        out_shape=jax.ShapeDtypeStruct(q.shape, BF16),
        scratch_shapes=[pltpu.VMEM((HEAD_PAIRS, Q_BLOCK, LANES), F32),
                        pltpu.VMEM((HEAD_PAIRS, Q_BLOCK, LANES), F32)],
        compiler_params=pltpu.CompilerParams(
            dimension_semantics=("arbitrary", "arbitrary"),
            vmem_limit_bytes=VMEM_LIMIT_BYTES),
        name="attention",
    )(q, k, v, scan_mat, head_keep)


def _scan_matrix():
    r = jnp.arange(2 * Q_BLOCK)
    same_head = (r[:, None] // Q_BLOCK) == (r[None, :] // Q_BLOCK)
    suffix = (same_head & (r[:, None] >= r[None, :])).astype(BF16)
    return jnp.concatenate([suffix, suffix], axis=0)


def _head_keep():
    first = (jnp.arange(LANES) < C_HEAD_DIM).astype(BF16)
    return jnp.broadcast_to(jnp.stack([first, 1 - first])[:, None, :], (2, Q_BLOCK, LANES))


def _mixer_out_kernel(x_ref, att_ref, part_ref, g2_ref, mod_ref, wc_ref, wo_ref, n2g_ref,
                      w1_ref, w2_ref, fg_ref, o_ref, *, ff_chunk, final):
    mod = mod_ref[0]
    gate1 = mod[:, 2 * D_MODEL:3 * D_MODEL]
    shift2 = mod[:, 3 * D_MODEL:4 * D_MODEL]
    scale2 = mod[:, 4 * D_MODEL:5 * D_MODEL]
    gate2 = mod[:, 5 * D_MODEL:6 * D_MODEL]
    att = jnp.concatenate([att_ref[0, p] for p in range(HEAD_PAIRS)], axis=1)
    merged = part_ref[0] + g2_ref[0] * _dot(att, wc_ref[0])
    x1 = x_ref[0] + (1.0 + gate1) * _dot(merged.astype(BF16), wo_ref[0])
    hb = (_rms_norm(x1, n2g_ref[0]) * (1.0 + scale2) + shift2).astype(BF16)
    y = None
    for c in range(D_FF // ff_chunk):
        cols = slice(c * ff_chunk, (c + 1) * ff_chunk)
        hid = jnp.square(jnp.maximum(_dot(hb, w1_ref[0, :, cols]), 0.0)).astype(BF16)
        part = _dot(hid, w2_ref[0, cols, :])
        y = part if y is None else y + part
    x2 = x1 + (1.0 + gate2) * y
    if final:
        x2 = _rms_norm(x2, fg_ref[...])
    o_ref[0] = x2


def _mixer_out_call(layer, x, att, part, g2, mod, wc, wo, n2g, w1, w2, fg, *, tm, final):
    bsz, seq, _ = x.shape
    lay3 = lambda b, s: (layer, 0, 0)
    tok = lambda b, s: (b, s, 0)
    return pl.pallas_call(
        functools.partial(_mixer_out_kernel, ff_chunk=D_MODEL, final=final),
        grid=(bsz, seq // tm),
        in_specs=[
            pl.BlockSpec((1, tm, D_MODEL), tok),
            pl.BlockSpec((1, HEAD_PAIRS, tm, LANES), lambda b, s: (b, 0, s, 0)),
            pl.BlockSpec((1, tm, D_MODEL), tok),
            pl.BlockSpec((1, tm, D_MODEL), tok),
            pl.BlockSpec((1, 1, N_MOD * D_MODEL), lambda b, s: (layer * SUBLANES + b, 0, 0)),
            _resident((1, C_WIDTH, D_MODEL), lay3),
            _resident((1, D_MODEL, D_MODEL), lay3),
            _resident((1, 1, D_MODEL), lay3),
            _resident((1, D_MODEL, D_FF), lay3),
            _resident((1, D_FF, D_MODEL), lay3),
            _resident((1, D_MODEL), lambda b, s: (0, 0)),
        ],
        out_specs=pl.BlockSpec((1, tm, D_MODEL), tok),
        out_shape=jax.ShapeDtypeStruct((bsz, seq, D_MODEL), F32),
        compiler_params=pltpu.CompilerParams(
            dimension_semantics=("arbitrary", "arbitrary"),
            vmem_limit_bytes=VMEM_LIMIT_BYTES),
        name="mixer_out",
    )(x, att, part, g2, mod, wc, wo, n2g, w1, w2, fg)


def kernel(x, c, ada_w, ada_b, norm1_g, w_in, a_ln_g, a_ln_b, a_ws, a_bs, b_conv_w,
           w_a_out, w_b_out, w_c_out, w_o, norm2_g, w_ff1, w_ff2, final_g):
    depth = ada_w.shape[0]
    bsz = x.shape[0]
    tm = 512
    c_pad = jnp.zeros((SUBLANES, D_MODEL), F32).at[:bsz].set(c)
    mod_all = _ada_call(c_pad, ada_w, ada_b).reshape(depth * SUBLANES, 1, N_MOD * D_MODEL)
    scan_mat = _scan_matrix()
    head_keep = _head_keep()
    fg = final_g.reshape(1, D_MODEL)
    n1g = norm1_g.reshape(depth, 1, D_MODEL)
    n2g = norm2_g.reshape(depth, 1, D_MODEL)
    lng = a_ln_g.reshape(depth, 1, A_WIDTH)
    lnb = a_ln_b.reshape(depth, 1, A_WIDTH)
    bsb = jnp.broadcast_to(a_bs[:, :, :, None], (depth, A_HEADS, CHUNK, A_WIDTH // A_HEADS))
    w_in_b, wa, wb, wc, wo, w1, w2 = (w.astype(BF16) for w in
                                      (w_in, w_a_out, w_b_out, w_c_out, w_o, w_ff1, w_ff2))
    for l in range(depth):
        q, k, v, part, g2 = _mixer_in_call(l, x, mod_all, n1g, w_in_b, lng, lnb, a_ws, bsb,
                                           b_conv_w, wa, wb, tm=tm)
        att = _attn_call(q, k, v, scan_mat, head_keep)
        x = _mixer_out_call(l, x, att, part, g2, mod_all, wc, wo, n2g, w1, w2, fg,
                            tm=tm, final=(l == depth - 1))
    return x
```

```python
import functools

import jax
import jax.numpy as jnp
from jax import lax
from jax.experimental import pallas as pl
from jax.experimental.pallas import tpu as pltpu

D_MODEL = 1024
N_MOD = 6
A_WIDTH = 512
A_HEADS = 4
CHUNK = 128
B_WIDTH = 512
CONV_W = 3
C_WIDTH = 512
C_HEAD_DIM = 64
Q_BLOCK = 128
D_FF = 4 * D_MODEL
EPS = 1e-6

LANES = 128
SUBLANES = 8
HEAD_PAIRS = C_WIDTH // LANES
VMEM_LIMIT_BYTES = 56 * 1024 * 1024
F32_EXP_UNDERFLOW = 104.0
BEFORE_SEQUENCE_START = 1e30

_OFF_AU = 0
_OFF_AV = _OFF_AU + A_WIDTH
_OFF_BB = _OFF_AV + A_WIDTH
_OFF_BC = _OFF_BB + B_WIDTH
_OFF_BX = _OFF_BC + B_WIDTH
_OFF_Q = _OFF_BX + B_WIDTH
_OFF_K = _OFF_Q + C_WIDTH
_OFF_V = _OFF_K + C_WIDTH
_OFF_G = _OFF_V + C_WIDTH

F32 = jnp.float32
BF16 = jnp.bfloat16


def _dot(a, b):
    return jnp.dot(a, b, preferred_element_type=F32)


def _rms_norm(x, g):
    return x * lax.rsqrt(jnp.mean(x * x, axis=-1, keepdims=True) + EPS) * g


def _resident(block_shape, index_map):
    return pl.BlockSpec(block_shape, index_map, pipeline_mode=pl.Buffered(1))


def _ada_kernel(c_ref, w_ref, b_ref, o_ref):
    c = c_ref[...]
    c_act = c * jax.nn.sigmoid(c)
    o_ref[0] = _dot(c_act.astype(BF16), w_ref[0].astype(BF16)) + b_ref[0]


def _ada_call(c_pad, ada_w, ada_b):
    depth = ada_w.shape[0]
    rows = c_pad.shape[0]
    return pl.pallas_call(
        _ada_kernel,
        grid=(depth, N_MOD),
        in_specs=[
            pl.BlockSpec((rows, D_MODEL), lambda l, j: (0, 0)),
            pl.BlockSpec((1, D_MODEL, D_MODEL), lambda l, j: (l, 0, j)),
            pl.BlockSpec((1, 1, D_MODEL), lambda l, j: (l, 0, j)),
        ],
        out_specs=pl.BlockSpec((1, rows, D_MODEL), lambda l, j: (l, 0, j)),
        out_shape=jax.ShapeDtypeStruct((depth, rows, N_MOD * D_MODEL), F32),
        compiler_params=pltpu.CompilerParams(
            dimension_semantics=("arbitrary", "arbitrary"),
            vmem_limit_bytes=VMEM_LIMIT_BYTES),
        name="adaln",
    )(c_pad, ada_w, ada_b.reshape(depth, 1, N_MOD * D_MODEL))


def _mixer_in_kernel(x_ref, mod_ref, n1g_ref, w_in_ref, lng_ref, lnb_ref, ws_ref, bsb_ref,
                     conv_ref, wa_ref, wb_ref,
                     q_ref, k_ref, v_ref, part_ref, g2_ref,
                     ya_ref, halo_ref, *, tm):
    s_blk = pl.program_id(1)
    x = x_ref[0]
    mod = mod_ref[0]
    shift1 = mod[:, 0:D_MODEL]
    scale1 = mod[:, D_MODEL:2 * D_MODEL]
    h = _rms_norm(x, n1g_ref[0]) * (1.0 + scale1) + shift1
    hb = h.astype(BF16)

    def proj(off, width):
        return _dot(hb, w_in_ref[0, :, off:off + width])

    a_u = proj(_OFF_AU, A_WIDTH)
    a_v = proj(_OFF_AV, A_WIDTH)
    b_b = proj(_OFF_BB, B_WIDTH)
    z = proj(_OFF_BC, B_WIDTH) * proj(_OFF_BX, B_WIDTH)

    u = jax.nn.gelu(a_u)
    v = jax.nn.gelu(a_v)
    mu = jnp.mean(v, axis=-1, keepdims=True)
    var = jnp.mean(jnp.square(v - mu), axis=-1, keepdims=True)
    v = ((v - mu) * lax.rsqrt(var + EPS) * lng_ref[0] + lnb_ref[0]).astype(BF16)
    t_idx = lax.broadcasted_iota(jnp.int32, (CHUNK, CHUNK), 0)
    s_idx = lax.broadcasted_iota(jnp.int32, (CHUNK, CHUNK), 1)
    causal = s_idx <= t_idx
    group = A_WIDTH // A_HEADS
    for hd in range(A_HEADS):
        w_h = jnp.where(causal, ws_ref[0, hd], 0.0).astype(BF16)
        cols = slice(hd * group, (hd + 1) * group)
        for ck in range(tm // CHUNK):
            rows = slice(ck * CHUNK, (ck + 1) * CHUNK)
            mixed = _dot(w_h, v[rows, cols]) + bsb_ref[0, hd]
            ya_ref[rows, cols] = (u[rows, cols] * mixed).astype(BF16)
    y_a = _dot(ya_ref[...], wa_ref[0])

    @pl.when(s_blk == 0)
    def _():
        halo_ref[...] = jnp.zeros_like(halo_ref)

    prev1 = halo_ref[SUBLANES - 1:SUBLANES, :]
    prev2 = halo_ref[SUBLANES - 2:SUBLANES - 1, :]
    row = lax.broadcasted_iota(jnp.int32, (tm, B_WIDTH), 0)
    z1 = jnp.where(row == 0, prev1, pltpu.roll(z, 1, 0))
    z2 = jnp.where(row == 0, prev2, jnp.where(row == 1, prev1, pltpu.roll(z, 2, 0)))
    halo_ref[...] = z[tm - SUBLANES:, :]
    conv = conv_ref[0, 0:1, :] * z2 + conv_ref[0, 1:2, :] * z1 + conv_ref[0, 2:3, :] * z
    y_b = _dot((b_b * conv).astype(BF16), wb_ref[0])

    q = proj(_OFF_Q, C_WIDTH) * (C_HEAD_DIM ** -0.5)
    k = proj(_OFF_K, C_WIDTH)
    vv = proj(_OFF_V, C_WIDTH)
    for p in range(HEAD_PAIRS):
        cols = slice(p * LANES, (p + 1) * LANES)
        q_ref[0, p] = q[:, cols].astype(BF16)
        k_ref[0, p] = k[:, cols].astype(BF16)
        v_ref[0, p] = vv[:, cols].astype(BF16)

    g0 = jax.nn.sigmoid(proj(_OFF_G, D_MODEL))
    g1 = jax.nn.sigmoid(proj(_OFF_G + D_MODEL, D_MODEL))
    part_ref[0] = (g0 * y_a + g1 * y_b).astype(BF16)
    g2_ref[0] = jax.nn.sigmoid(proj(_OFF_G + 2 * D_MODEL, D_MODEL)).astype(BF16)


def _mixer_in_call(layer, x, mod, n1g, w_in, lng, lnb, ws, bsb, conv_w, wa, wb, *, tm):
    bsz, seq, _ = x.shape
    in_cols = w_in.shape[-1]
    lay3 = lambda b, s: (layer, 0, 0)
    lay4 = lambda b, s: (layer, 0, 0, 0)
    tok = lambda b, s: (b, s, 0)
    hp = lambda b, s: (b, 0, s, 0)
    qkv_shape = jax.ShapeDtypeStruct((bsz, HEAD_PAIRS, seq, LANES), BF16)
    return pl.pallas_call(
        functools.partial(_mixer_in_kernel, tm=tm),
        grid=(bsz, seq // tm),
        in_specs=[
            pl.BlockSpec((1, tm, D_MODEL), tok),
            pl.BlockSpec((1, 1, N_MOD * D_MODEL), lambda b, s: (layer * SUBLANES + b, 0, 0)),
            _resident((1, 1, D_MODEL), lay3),
            _resident((1, D_MODEL, in_cols), lay3),
            _resident((1, 1, A_WIDTH), lay3),
            _resident((1, 1, A_WIDTH), lay3),
            _resident((1, A_HEADS, CHUNK, CHUNK), lay4),
            _resident((1, A_HEADS, CHUNK, A_WIDTH // A_HEADS), lay4),
            _resident((1, CONV_W, B_WIDTH), lay3),
            _resident((1, A_WIDTH, D_MODEL), lay3),
            _resident((1, B_WIDTH, D_MODEL), lay3),
        ],
        out_specs=[
            pl.BlockSpec((1, HEAD_PAIRS, tm, LANES), hp),
            pl.BlockSpec((1, HEAD_PAIRS, tm, LANES), hp),
            pl.BlockSpec((1, HEAD_PAIRS, tm, LANES), hp),
            pl.BlockSpec((1, tm, D_MODEL), tok),
            pl.BlockSpec((1, tm, D_MODEL), tok),
        ],
        out_shape=[qkv_shape, qkv_shape, qkv_shape,
                   jax.ShapeDtypeStruct((bsz, seq, D_MODEL), BF16),
                   jax.ShapeDtypeStruct((bsz, seq, D_MODEL), BF16)],
        scratch_shapes=[pltpu.VMEM((tm, A_WIDTH), BF16),
                        pltpu.VMEM((SUBLANES, B_WIDTH), F32)],
        compiler_params=pltpu.CompilerParams(
            dimension_semantics=("arbitrary", "arbitrary"),
            vmem_limit_bytes=VMEM_LIMIT_BYTES),
        name="mixer_in",
    )(x, mod, n1g, w_in, lng, lnb, ws, bsb, conv_w, wa, wb)


PROLOGUE_BLOCKS = 3
Q_BLOCKS_PER_STEP = 8


def _attn_kernel(q_ref, k_ref, v_ref, scan_ref, keep_ref, o_ref, acc_ref, across_ref):
    first_head = lax.broadcasted_iota(jnp.int32, (Q_BLOCK, LANES), 1) < C_HEAD_DIM
    row = lax.broadcasted_iota(jnp.int32, (Q_BLOCK, 2 * Q_BLOCK), 0)
    col = lax.broadcasted_iota(jnp.int32, (Q_BLOCK, 2 * Q_BLOCK), 1) & (Q_BLOCK - 1)
    diag_mask = col < row

    def rows_of(j):
        return pl.ds(pl.multiple_of(j * Q_BLOCK, Q_BLOCK), Q_BLOCK)

    def head_stack(ref, p, j):
        blk = ref[0, p, rows_of(j), :]
        return jnp.concatenate([blk * keep_ref[0], blk * keep_ref[1]], axis=0)

    def visit(qi, blocks):
        zs, sps = [], []
        for j, diag in blocks:
            for p in range(HEAD_PAIRS):
                z = lax.dot_general(q_ref[0, p, rows_of(qi), :], head_stack(k_ref, p, j),
                                    (((1,), (1,)), ((), ())), preferred_element_type=F32)
                sp = jnp.maximum(z, 0.0) + jnp.log(1.0 + jnp.exp(-jnp.abs(z)))
                zs.append(z)
                sps.append(jnp.where(diag_mask, sp, 0.0) if diag else sp)
        sp_all = jnp.concatenate(sps, axis=0)
        hi = sp_all.astype(BF16)
        lo = (sp_all - hi.astype(F32)).astype(BF16)
        incl = _dot(jnp.concatenate([hi, lo], axis=1), scan_ref[...])
        outs = []
        for b, (j, diag) in enumerate(blocks):
            per_pair = []
            for p in range(HEAD_PAIRS):
                n = b * HEAD_PAIRS + p
                a = jnp.exp(zs[n] - incl[n * Q_BLOCK:(n + 1) * Q_BLOCK])
                if diag:
                    a = jnp.where(diag_mask, a, 0.0)
                tot = jnp.where(first_head,
                                jnp.sum(sps[n][:, :Q_BLOCK], axis=-1, keepdims=True),
                                jnp.sum(sps[n][:, Q_BLOCK:], axis=-1, keepdims=True))
                per_pair.append((_dot(a.astype(BF16), head_stack(v_ref, p, j)), tot))
            outs.append(per_pair)
        return outs

    def all_underflowed(across):
        lowest = functools.reduce(jnp.minimum, across)
        return (jnp.min(lowest) > F32_EXP_UNDERFLOW).astype(jnp.int32)

    def query_block(qi, carry):
        i = pl.program_id(1) * Q_BLOCKS_PER_STEP + qi
        visits = visit(qi, [(i, True)] + [(jnp.maximum(i - d, 0), False)
                                          for d in range(1, PROLOGUE_BLOCKS)])
        across_all = []
        for p in range(HEAD_PAIRS):
            acc, across = visits[0][p]
            for d in range(1, PROLOGUE_BLOCKS):
                av, tot = visits[d][p]
                across = across + jnp.where(i >= d, 0.0, BEFORE_SEQUENCE_START)
                acc = acc + jnp.exp(-across) * av
                across = across + tot
            acc_ref[p] = acc
            across_ref[p] = across
            across_all.append(across)

        def body(c):
            j, _ = c
            across_all = []
            for p, (av, tot) in enumerate(visit(qi, [(j, False)])[0]):
                across = across_ref[p]
                acc_ref[p] += jnp.exp(-across) * av
                across_ref[p] = across + tot
                across_all.append(across + tot)
            return j - 1, all_underflowed(across_all)

        lax.while_loop(lambda c: (c[0] >= 0) & (c[1] == 0), body,
                       (i - PROLOGUE_BLOCKS, all_underflowed(across_all)))
        for p in range(HEAD_PAIRS):
            o_ref[0, p, rows_of(qi), :] = acc_ref[p].astype(BF16)
        return carry

    lax.fori_loop(0, Q_BLOCKS_PER_STEP, query_block, 0)


def _attn_call(q, k, v, scan_mat, head_keep):
    bsz, _, seq, _ = q.shape
    rows = Q_BLOCKS_PER_STEP * Q_BLOCK
    blk = lambda b, i: (b, 0, i, 0)
    whole = lambda b, i: (b, 0, 0, 0)
    return pl.pallas_call(
        _attn_kernel,
        grid=(bsz, seq // rows),
        in_specs=[
            pl.BlockSpec((1, HEAD_PAIRS, rows, LANES), blk),
            _resident((1, HEAD_PAIRS, seq, LANES), whole),
            _resident((1, HEAD_PAIRS, seq, LANES), whole),
            _resident(scan_mat.shape, lambda b, i: (0, 0)),
            _resident(head_keep.shape, lambda b, i: (0, 0, 0)),
        ],
        out_specs=pl.BlockSpec((1, HEAD_PAIRS, rows, LANES), blk),
        out_shape=jax.ShapeDtypeStruct(q.shape, BF16),
        scratch_shapes=[pltpu.VMEM((HEAD_PAIRS, Q_BLOCK, LANES), F32),
                        pltpu.VMEM((HEAD_PAIRS, Q_BLOCK, LANES), F32)],
        compiler_params=pltpu.CompilerParams(
            dimension_semantics=("arbitrary", "arbitrary"),
            vmem_limit_bytes=VMEM_LIMIT_BYTES),
        name="attention",
    )(q, k, v, scan_mat, head_keep)


def _scan_matrix():
    r = jnp.arange(2 * Q_BLOCK)
    same_head = (r[:, None] // Q_BLOCK) == (r[None, :] // Q_BLOCK)
    suffix = (same_head & (r[:, None] >= r[None, :])).astype(BF16)
    return jnp.concatenate([suffix, suffix], axis=0)


def _head_keep():
    first = (jnp.arange(LANES) < C_HEAD_DIM).astype(BF16)
    return jnp.broadcast_to(jnp.stack([first, 1 - first])[:, None, :], (2, Q_BLOCK, LANES))


def _mixer_out_kernel(x_ref, att_ref, part_ref, g2_ref, mod_ref, wc_ref, wo_ref, n2g_ref,
                      w1_ref, w2_ref, fg_ref, o_ref, *, ff_chunk, final):
    mod = mod_ref[0]
    gate1 = mod[:, 2 * D_MODEL:3 * D_MODEL]
    shift2 = mod[:, 3 * D_MODEL:4 * D_MODEL]
    scale2 = mod[:, 4 * D_MODEL:5 * D_MODEL]
    gate2 = mod[:, 5 * D_MODEL:6 * D_MODEL]
    att = jnp.concatenate([att_ref[0, p] for p in range(HEAD_PAIRS)], axis=1)
    merged = part_ref[0] + g2_ref[0] * _dot(att, wc_ref[0])
    x1 = x_ref[0] + (1.0 + gate1) * _dot(merged.astype(BF16), wo_ref[0])
    hb = (_rms_norm(x1, n2g_ref[0]) * (1.0 + scale2) + shift2).astype(BF16)
    y = None
    for c in range(D_FF // ff_chunk):
        cols = slice(c * ff_chunk, (c + 1) * ff_chunk)
        hid = jnp.square(jnp.maximum(_dot(hb, w1_ref[0, :, cols]), 0.0)).astype(BF16)
        part = _dot(hid, w2_ref[0, cols, :])
        y = part if y is None else y + part
    x2 = x1 + (1.0 + gate2) * y
    if final:
        x2 = _rms_norm(x2, fg_ref[...])
    o_ref[0] = x2


def _mixer_out_call(layer, x, att, part, g2, mod, wc, wo, n2g, w1, w2, fg, *, tm, final):
    bsz, seq, _ = x.shape
    lay3 = lambda b, s: (layer, 0, 0)
    tok = lambda b, s: (b, s, 0)
    return pl.pallas_call(
        functools.partial(_mixer_out_kernel, ff_chunk=D_MODEL, final=final),
        grid=(bsz, seq // tm),
        in_specs=[
            pl.BlockSpec((1, tm, D_MODEL), tok),
            pl.BlockSpec((1, HEAD_PAIRS, tm, LANES), lambda b, s: (b, 0, s, 0)),
            pl.BlockSpec((1, tm, D_MODEL), tok),
            pl.BlockSpec((1, tm, D_MODEL), tok),
            pl.BlockSpec((1, 1, N_MOD * D_MODEL), lambda b, s: (layer * SUBLANES + b, 0, 0)),
            _resident((1, C_WIDTH, D_MODEL), lay3),
            _resident((1, D_MODEL, D_MODEL), lay3),
            _resident((1, 1, D_MODEL), lay3),
            _resident((1, D_MODEL, D_FF), lay3),
            _resident((1, D_FF, D_MODEL), lay3),
            _resident((1, D_MODEL), lambda b, s: (0, 0)),
        ],
        out_specs=pl.BlockSpec((1, tm, D_MODEL), tok),
        out_shape=jax.ShapeDtypeStruct((bsz, seq, D_MODEL), F32),
        compiler_params=pltpu.CompilerParams(
            dimension_semantics=("arbitrary", "arbitrary"),
            vmem_limit_bytes=VMEM_LIMIT_BYTES),
        name="mixer_out",
    )(x, att, part, g2, mod, wc, wo, n2g, w1, w2, fg)


def kernel(x, c, ada_w, ada_b, norm1_g, w_in, a_ln_g, a_ln_b, a_ws, a_bs, b_conv_w,
           w_a_out, w_b_out, w_c_out, w_o, norm2_g, w_ff1, w_ff2, final_g):
    depth = ada_w.shape[0]
    bsz = x.shape[0]
    tm_in, tm_out = 1024, 512
    c_pad = jnp.zeros((SUBLANES, D_MODEL), F32).at[:bsz].set(c)
    mod_all = _ada_call(c_pad, ada_w, ada_b).reshape(depth * SUBLANES, 1, N_MOD * D_MODEL)
    scan_mat = _scan_matrix()
    head_keep = _head_keep()
    fg = final_g.reshape(1, D_MODEL)
    n1g = norm1_g.reshape(depth, 1, D_MODEL)
    n2g = norm2_g.reshape(depth, 1, D_MODEL)
    lng = a_ln_g.reshape(depth, 1, A_WIDTH)
    lnb = a_ln_b.reshape(depth, 1, A_WIDTH)
    bsb = jnp.broadcast_to(a_bs[:, :, :, None], (depth, A_HEADS, CHUNK, A_WIDTH // A_HEADS))
    w_in_b, wa, wb, wc, wo, w1, w2 = (w.astype(BF16) for w in
                                      (w_in, w_a_out, w_b_out, w_c_out, w_o, w_ff1, w_ff2))
    for l in range(depth):
        q, k, v, part, g2 = _mixer_in_call(l, x, mod_all, n1g, w_in_b, lng, lnb, a_ws, bsb,
                                           b_conv_w, wa, wb, tm=tm_in)
        att = _attn_call(q, k, v, scan_mat, head_keep)
        x = _mixer_out_call(l, x, att, part, g2, mod_all, wc, wo, n2g, w1, w2, fg,
                            tm=tm_out, final=(l == depth - 1))
    return x
```
